```python
import math
import jax
import jax.numpy as jnp
from jax import lax
import numpy as np

D_MODEL = 1024
BATCH = 8
SEQ = 2048
DEPTH = 4
DEC_BATCH = 128
DEC_SEQ = 4
PAST_LEN = 16384
PAGE_SIZE = 128

MIX_W = D_MODEL
N_MIXERS = 4
BRANCH_W = MIX_W // N_MIXERS
S5_CH = 16
S5_GROUPS = BRANCH_W // S5_CH
S5_STATE = 64
HGRN_HEADS = 4
HGRN_DK = 64
HGRN_DV = BRANCH_W // HGRN_HEADS
MLSTM_HEADS = 4
MLSTM_DH = BRANCH_W // MLSTM_HEADS
RET_HEADS = 4
RET_DK = 32
RET_DV = BRANCH_W // RET_HEADS
ROPE_BASE = 10000.0
CHUNK = 64
EPS = 1e-6
LOGI_PAD = -1e30

PROJ_SIZES = (
    BRANCH_W, BRANCH_W,
    HGRN_HEADS * HGRN_DK, HGRN_HEADS * HGRN_DK, BRANCH_W, BRANCH_W,
    BRANCH_W, BRANCH_W, BRANCH_W, BRANCH_W, MLSTM_HEADS, MLSTM_HEADS, BRANCH_W,
    RET_HEADS * RET_DK, RET_HEADS * RET_DK, BRANCH_W, BRANCH_W,
)
D_IN = sum(PROJ_SIZES)

kernel_name = 'hymba_style_s5_hgrn2_mlstm_retnet_step'


def rmsnorm(x, w):
    xf = x.astype(jnp.float32)
    xf = xf * lax.rsqrt(jnp.mean(xf * xf, axis=-1, keepdims=True) + EPS)
    return (xf * w.astype(jnp.float32)).astype(x.dtype)


def head_norm(o):
    mu = jnp.mean(o, axis=-1, keepdims=True)
    var = jnp.mean(jnp.square(o - mu), axis=-1, keepdims=True)
    return (o - mu) * lax.rsqrt(var + EPS)


def to_heads(t, n_heads):
    b, l, _ = t.shape
    return t.reshape(b, l, n_heads, -1).transpose(0, 2, 1, 3)


def from_heads(t):
    b, h, l, d = t.shape
    return t.transpose(0, 2, 1, 3).reshape(b, l, h * d)


def pad_time(t, pad, value=0.0):
    widths = [(0, 0)] * t.ndim
    widths[2] = (0, pad)
    return jnp.pad(t, widths, constant_values=value)


def to_blocks(t, n_blk, c):
    b, h = t.shape[:2]
    return jnp.moveaxis(t.reshape((b, h, n_blk, c) + t.shape[3:]), 2, 0)


def rotary(t, pos):
    half = t.shape[-1] // 2
    inv_freq = ROPE_BASE ** (-jnp.arange(half, dtype=jnp.float32) / half)
    ang = pos.astype(jnp.float32)[:, None] * inv_freq[None, :]
    cos, sin = jnp.cos(ang), jnp.sin(ang)
    t1, t2 = t[..., :half], t[..., half:]
    return jnp.concatenate([t1 * cos - t2 * sin, t1 * sin + t2 * cos], axis=-1)


def chunked_gated_recurrence(q, k, v, log_f, s0):
    b_, h_, l_, _ = q.shape
    c = min(CHUNK, l_)
    n_blk = -(-l_ // c)
    pad = n_blk * c - l_
    if pad:
        q, k, v, log_f = (pad_time(t, pad) for t in (q, k, v, log_f))
    scalar_decay = log_f.shape[-1] == 1
    idx = jnp.arange(c)
    causal = idx[:, None] >= idx[None, :]

    def step(s, inp):
        qc, kc, vc, gc = inp
        cum = jnp.cumsum(gc, axis=2)
        inter = jnp.einsum('bhtk,bhkv->bhtv', qc * jnp.exp(cum), s)
        rel = cum[:, :, :, None, :] - cum[:, :, None, :, :]
        decay = jnp.exp(jnp.where(causal[:, :, None], rel, -jnp.inf))
        if scalar_decay:
            scores = jnp.einsum('bhtk,bhsk->bhts', qc, kc) * decay[..., 0]
        else:
            scores = jnp.einsum('bhtk,bhtsk,bhsk->bhts', qc, decay, kc)
        o = inter + jnp.einsum('bhts,bhsv->bhtv', scores, vc)
        last = cum[:, :, -1:, :]
        s = jnp.exp(last[:, :, 0, :])[..., None] * s + jnp.einsum('bhsk,bhsv->bhkv', kc * jnp.exp(last - cum), vc)
        return s, o

    s_fin, o = lax.scan(step, s0, tuple(to_blocks(t, n_blk, c) for t in (q, k, v, log_f)))
    o = jnp.moveaxis(o, 0, 2).reshape(b_, h_, n_blk * c, -1)[:, :, :l_]
    return o, s_fin


def mlstm_chunked(q, k, v, log_i, log_f, c0, n0, m0):
    b_, h_, l_, _ = q.shape
    c = min(CHUNK, l_)
    n_blk = -(-l_ // c)
    pad = n_blk * c - l_
    if pad:
        q, k, v, log_f = (pad_time(t, pad) for t in (q, k, v, log_f))
        log_i = pad_time(log_i, pad, LOGI_PAD)
    idx = jnp.arange(c)
    causal = idx[:, None] >= idx[None, :]

    def step(carry, inp):
        cs, ns, ms = carry
        qc, kc, vc, ic, fc = inp
        cum = jnp.cumsum(fc, axis=-1)
        w_prev = cum + ms[..., None]
        log_d = jnp.where(causal, cum[..., :, None] - cum[..., None, :] + ic[..., None, :], -jnp.inf)
        m_t = jnp.maximum(w_prev, jnp.max(log_d, axis=-1))
        d_mat = jnp.exp(log_d - m_t[..., None])
        s_prev = jnp.exp(w_prev - m_t)
        scores = jnp.einsum('bhtk,bhsk->bhts', qc, kc) * d_mat
        num = s_prev[..., None] * jnp.einsum('bhtk,bhkv->bhtv', qc, cs) + jnp.einsum('bhts,bhsv->bhtv', scores, vc)
        den = s_prev * jnp.einsum('bhtk,bhk->bht', qc, ns) + jnp.sum(scores, axis=-1)
        h = num / jnp.maximum(jnp.abs(den), jnp.exp(-m_t))[..., None]
        m_new = m_t[..., -1]
        w_k = jnp.exp(cum[..., -1:] - cum + ic - m_new[..., None])
        carry_scale = jnp.exp(cum[..., -1] + ms - m_new)
        cs = carry_scale[..., None, None] * cs + jnp.einsum('bhs,bhsk,bhsv->bhkv', w_k, kc, vc)
        ns = carry_scale[..., None] * ns + jnp.einsum('bhs,bhsk->bhk', w_k, kc)
        return (cs, ns, m_new), h

    (c_fin, n_fin, m_fin), h = lax.scan(
        step, (c0, n0, m0), tuple(to_blocks(t, n_blk, c) for t in (q, k, v, log_i, log_f)))
    h = jnp.moveaxis(h, 0, 2).reshape(b_, h_, n_blk * c, -1)[:, :, :l_]
    return h, c_fin, n_fin, m_fin


def s5_mixer(u, h0_re, h0_im, lam_re, lam_im, b_re, b_im, c_re, c_im, d_skip, log_dt, w_glu):
    bsz, l_, _ = u.shape
    f32 = jnp.float32
    lam = lax.complex(lam_re.astype(f32), lam_im.astype(f32))
    dt = jnp.exp(log_dt.astype(f32))[:, None]
    a_bar = jnp.exp(lam * dt)
    b_bar = ((a_bar - 1.0) / lam)[:, :, None] * lax.complex(b_re.astype(f32), b_im.astype(f32))
    c_mat = lax.complex(c_re.astype(f32), c_im.astype(f32))
    ug = u.reshape(bsz, l_, S5_GROUPS, S5_CH).astype(jnp.complex64)
    bu = jnp.einsum('gpc,blgc->blgp', b_bar, ug)
    bu = bu.at[:, 0].add(a_bar * lax.complex(h0_re.astype(f32), h0_im.astype(f32)))
    a_all = jnp.broadcast_to(a_bar, bu.shape)

    def combine(e1, e2):
        a1, x1 = e1
        a2, x2 = e2
        return a2 * a1, a2 * x1 + x2

    _, h = lax.associative_scan(combine, (a_all, bu), axis=1)
    y = jnp.real(jnp.einsum('gcp,blgp->blgc', c_mat, h)).reshape(bsz, l_, BRANCH_W) + d_skip.astype(f32) * u
    y = jax.nn.gelu(y)
    y = y * jax.nn.sigmoid(y @ w_glu.astype(f32))
    return y, jnp.real(h[:, -1]), jnp.imag(h[:, -1])


def trunk(x, pos, states, params):
    s5_re, s5_im, hg_s, ml_c, ml_n, ml_m, rt_s = states
    (norm_w, w_in, w_out, final_norm_w, lam_re, lam_im, b_re, b_im, c_re, c_im,
     s5_d, s5_log_dt, s5_w_glu, hgrn_lb_logits, mlstm_b_i, mlstm_b_f) = params
    f32 = jnp.float32
    split_points = [int(v) for v in np.cumsum(PROJ_SIZES)[:-1]]
    lower_bounds = jnp.cumsum(jax.nn.softmax(hgrn_lb_logits.astype(f32), axis=0), axis=0)
    lower_bounds = lower_bounds - lower_bounds[0]
    ret_log_decay = jnp.log(1.0 - 2.0 ** (-5.0 - jnp.arange(RET_HEADS, dtype=f32)))
    new = [[] for _ in range(7)]
    for l in range(DEPTH):
        h = rmsnorm(x, norm_w[l])
        proj = jnp.einsum('bld,de->ble', h, w_in[l]).astype(f32)
        (s5_x, s5_g, hg_q, hg_f, hg_i, hg_g, ml_q, ml_k, ml_v, ml_o, ml_i, ml_f, ml_g,
         rt_q, rt_k, rt_v, rt_g) = jnp.split(proj, split_points, axis=-1)

        s5_out, s5_hre, s5_him = s5_mixer(s5_x, s5_re[l], s5_im[l], lam_re[l], lam_im[l], b_re[l], b_im[l],
                                          c_re[l], c_im[l], s5_d[l], s5_log_dt[l], s5_w_glu[l])

        lb = lower_bounds[l].reshape(HGRN_HEADS, 1, HGRN_DK)
        hq = jax.nn.silu(to_heads(hg_q, HGRN_HEADS))
        log_f = jnp.logaddexp(jnp.log(lb), jnp.log1p(-lb) + jax.nn.log_sigmoid(to_heads(hg_f, HGRN_HEADS)))
        hk = -jnp.expm1(log_f)
        ho, hg_new = chunked_gated_recurrence(hq, hk, to_heads(hg_i, HGRN_HEADS), log_f, hg_s[l].astype(f32))
        hg_out = from_heads(head_norm(ho))

        mq = to_heads(ml_q, MLSTM_HEADS)
        mk = to_heads(ml_k, MLSTM_HEADS) * (MLSTM_DH ** -0.5)
        mv = to_heads(ml_v, MLSTM_HEADS)
        log_ig = (ml_i + mlstm_b_i[l].astype(f32)).transpose(0, 2, 1)
        log_fg = jax.nn.log_sigmoid(ml_f + mlstm_b_f[l].astype(f32)).transpose(0, 2, 1)
        mh, mc_new, mn_new, mm_new = mlstm_chunked(mq, mk, mv, log_ig, log_fg, ml_c[l].astype(f32),
                                                   ml_n[l].astype(f32), ml_m[l].astype(f32))
        ml_out = from_heads(head_norm(jax.nn.sigmoid(to_heads(ml_o, MLSTM_HEADS)) * mh))

        rq = rotary(to_heads(rt_q, RET_HEADS), pos)
        rk = rotary(to_heads(rt_k, RET_HEADS), pos) * (RET_DK ** -0.5)
        rv = to_heads(rt_v, RET_HEADS)
        log_g = jnp.broadcast_to(ret_log_decay[None, :, None, None], rq.shape[:3] + (1,))
        ro, rt_new = chunked_gated_recurrence(rq, rk, rv, log_g, rt_s[l].astype(f32))
        rt_out = from_heads(head_norm(ro))

        mixed = jnp.concatenate([s5_out * jax.nn.silu(s5_g), hg_out * jax.nn.silu(hg_g),
                                 ml_out * jax.nn.silu(ml_g), rt_out * jax.nn.silu(rt_g)], axis=-1)
        x = x + jnp.einsum('ble,ed->bld', mixed.astype(x.dtype), w_out[l])
        for lst, val in zip(new, (s5_hre, s5_him, hg_new, mc_new, mn_new, mm_new, rt_new)):
            lst.append(val)
    y = rmsnorm(x, final_norm_w)
    return y, [jnp.stack(v, axis=0) for v in new]


def setup_inputs(seed: int = 0) -> dict:
    key = jax.random.key(seed)
    ks = jax.random.split(key, 24)
    f32 = jnp.float32

    def nrm(k, shape, s):
        return jax.random.normal(k, shape, f32) * s

    return {
        'x_prompt': nrm(ks[0], (BATCH, SEQ, D_MODEL), 1.0),
        'x_sample': nrm(ks[1], (DEC_BATCH, DEC_SEQ, D_MODEL), 1.0),
        'state_s5_re': nrm(ks[2], (DEPTH, DEC_BATCH, S5_GROUPS, S5_STATE), 0.5),
        'state_s5_im': nrm(ks[3], (DEPTH, DEC_BATCH, S5_GROUPS, S5_STATE), 0.5),
        'state_hgrn': nrm(ks[4], (DEPTH, DEC_BATCH, HGRN_HEADS, HGRN_DK, HGRN_DV), 0.5),
        'state_mlstm_C': nrm(ks[5], (DEPTH, DEC_BATCH, MLSTM_HEADS, MLSTM_DH, MLSTM_DH), 0.5),
        'state_mlstm_n': nrm(ks[6], (DEPTH, DEC_BATCH, MLSTM_HEADS, MLSTM_DH), 0.5),
        'state_mlstm_m': nrm(ks[7], (DEPTH, DEC_BATCH, MLSTM_HEADS), 1.0),
        'state_ret': nrm(ks[8], (DEPTH, DEC_BATCH, RET_HEADS, RET_DK, RET_DV), 0.5),
        'norm_w': 1.0 + nrm(ks[9], (DEPTH, D_MODEL), 0.02),
        'w_in': nrm(ks[10], (DEPTH, D_MODEL, D_IN), D_MODEL ** -0.5),
        'w_out': nrm(ks[11], (DEPTH, MIX_W, D_MODEL), 0.5 * MIX_W ** -0.5),
        'final_norm_w': 1.0 + nrm(ks[12], (D_MODEL,), 0.02),
        's5_lambda_re': -0.5 + nrm(ks[13], (DEPTH, S5_GROUPS, S5_STATE), 0.01),
        's5_lambda_im': math.pi * jnp.arange(S5_STATE, dtype=f32) + nrm(ks[14], (DEPTH, S5_GROUPS, S5_STATE), 0.01),
        's5_B_re': nrm(ks[15], (DEPTH, S5_GROUPS, S5_STATE, S5_CH), (2 * S5_CH) ** -0.5),
        's5_B_im': nrm(ks[16], (DEPTH, S5_GROUPS, S5_STATE, S5_CH), (2 * S5_CH) ** -0.5),
        's5_C_re': nrm(ks[17], (DEPTH, S5_GROUPS, S5_CH, S5_STATE), (2 * S5_STATE) ** -0.5),
        's5_C_im': nrm(ks[18], (DEPTH, S5_GROUPS, S5_CH, S5_STATE), (2 * S5_STATE) ** -0.5),
        's5_D': nrm(ks[19], (DEPTH, BRANCH_W), 1.0),
        's5_log_dt': jax.random.uniform(ks[20], (DEPTH, S5_GROUPS), f32, math.log(1e-3), math.log(1e-1)),
        's5_w_glu': nrm(ks[21], (DEPTH, BRANCH_W, BRANCH_W), BRANCH_W ** -0.5),
        'hgrn_lb_logits': nrm(ks[22], (DEPTH, HGRN_HEADS * HGRN_DK), 1.0),
        'mlstm_b_i': nrm(ks[23], (DEPTH, MLSTM_HEADS), 0.1),
        'mlstm_b_f': jnp.linspace(3.0, 6.0, MLSTM_HEADS, dtype=f32)[None, :] + nrm(jax.random.fold_in(ks[23], 1), (DEPTH, MLSTM_HEADS), 0.1),
    }


def reference(x_prompt, x_sample, state_s5_re, state_s5_im, state_hgrn, state_mlstm_C, state_mlstm_n,
              state_mlstm_m, state_ret, norm_w, w_in, w_out, final_norm_w, s5_lambda_re, s5_lambda_im,
              s5_B_re, s5_B_im, s5_C_re, s5_C_im, s5_D, s5_log_dt, s5_w_glu, hgrn_lb_logits,
              mlstm_b_i, mlstm_b_f):
    f32 = jnp.float32
    params = (norm_w, w_in, w_out, final_norm_w, s5_lambda_re, s5_lambda_im, s5_B_re, s5_B_im,
              s5_C_re, s5_C_im, s5_D, s5_log_dt, s5_w_glu, hgrn_lb_logits, mlstm_b_i, mlstm_b_f)
    bp, lp = x_prompt.shape[0], x_prompt.shape[1]
    zero_states = (
        jnp.zeros((DEPTH, bp, S5_GROUPS, S5_STATE), f32),
        jnp.zeros((DEPTH, bp, S5_GROUPS, S5_STATE), f32),
        jnp.zeros((DEPTH, bp, HGRN_HEADS, HGRN_DK, HGRN_DV), f32),
        jnp.zeros((DEPTH, bp, MLSTM_HEADS, MLSTM_DH, MLSTM_DH), f32),
        jnp.zeros((DEPTH, bp, MLSTM_HEADS, MLSTM_DH), f32),
        jnp.zeros((DEPTH, bp, MLSTM_HEADS), f32),
        jnp.zeros((DEPTH, bp, RET_HEADS, RET_DK, RET_DV), f32),
    )
    pos_prompt = jnp.arange(lp, dtype=jnp.int32)
    pos_sample = PAST_LEN + jnp.arange(x_sample.shape[1], dtype=jnp.int32)
    y_prompt, (p_s5_re, p_s5_im, p_hgrn, p_mlstm_C, p_mlstm_n, p_mlstm_m, p_ret) = trunk(
        x_prompt, pos_prompt, zero_states, params)
    sample_states = (state_s5_re, state_s5_im, state_hgrn, state_mlstm_C, state_mlstm_n, state_mlstm_m, state_ret)
    y_sample, (d_s5_re, d_s5_im, d_hgrn, d_mlstm_C, d_mlstm_n, d_mlstm_m, d_ret) = trunk(
        x_sample, pos_sample, sample_states, params)
    return (y_prompt, y_sample,
            p_s5_re, p_s5_im, p_hgrn, p_mlstm_C, p_mlstm_n, p_mlstm_m, p_ret,
            d_s5_re, d_s5_im, d_hgrn, d_mlstm_C, d_mlstm_n, d_mlstm_m, d_ret)
```

```python
import functools
import math

import numpy as np
import jax
import jax.numpy as jnp
from jax import lax
from jax.experimental import pallas as pl
from jax.experimental.pallas import tpu as pltpu

F32 = jnp.float32
BF16 = jnp.bfloat16

D_MODEL = 1024
DEPTH = 4
BRANCH_W = 256
N_HEADS = 4
HEAD_W = 64
S5_GROUPS = 16
S5_CH = 16
S5_STATE = 64
S5_N = S5_GROUPS * S5_STATE
S5_TILES = S5_N // 128
RET_DK = 32
RET_KW = N_HEADS * RET_DK
ROPE_BASE = 10000.0
EPS = 1e-6
LOGI_PAD = -1e30
PAST_LEN = 16384

CHUNK_ROWS = 128
STACK = N_HEADS * CHUNK_ROWS
PROMPT_T = 64
VMEM_LIMIT_BYTES = 60 * 1024 * 1024

C_S5 = 0
C_HG = 512
C_ML = 1536
C_RT = 2816
C_GATE = 3584
W_ALL_COLS = 3712
PJ_COLS = 1408


def _dot(a, b):
    return jnp.dot(a, b, preferred_element_type=F32)


def _dot_nt(a, b):
    return lax.dot_general(a, b, (((1,), (1,)), ((), ())), preferred_element_type=F32)


def _dot_tn(a, b):
    return lax.dot_general(a, b, (((0,), (0,)), ((), ())), preferred_element_type=F32)


def _split3(x):
    hi = x.astype(BF16)
    r = x - hi.astype(F32)
    mid = r.astype(BF16)
    lo = (r - mid.astype(F32)).astype(BF16)
    return hi, mid, lo


def _sel_dot(sel_bf, x):
    hi, mid, lo = _split3(x)
    return _dot(sel_bf, hi) + _dot(sel_bf, mid) + _dot(sel_bf, lo)


def _dot_sel(x, sel_bf):
    hi = x.astype(BF16)
    lo = (x - hi.astype(F32)).astype(BF16)
    return _dot(hi, sel_bf) + _dot(lo, sel_bf)


def _sigmoid(x):
    return 1.0 / (1.0 + jnp.exp(-x))


def _silu(x):
    return x * _sigmoid(x)


def _log_sigmoid(x):
    return jnp.minimum(x, 0.0) - jnp.log1p(jnp.exp(-jnp.abs(x)))


def _gelu_tanh(x):
    return 0.5 * x * (1.0 + jnp.tanh(math.sqrt(2.0 / math.pi) * (x + 0.044715 * (x * x * x))))


def _iota(shape, dim):
    return lax.broadcasted_iota(jnp.int32, shape, dim)


def _log2(n):
    k = int(round(math.log2(n)))
    assert (1 << k) == n, n
    return k


def _stack_heads(x_bf, head_mask):
    return jnp.where(head_mask, jnp.concatenate([x_bf] * N_HEADS, axis=0), jnp.zeros((), x_bf.dtype))


def _tile_state(s, mask):
    return jnp.where(mask, jnp.concatenate([s] * N_HEADS, axis=1), 0.0)


def _untile_state(s_bd):
    return (s_bd[:, 0:64] + s_bd[:, 64:128]) + (s_bd[:, 128:192] + s_bd[:, 192:256])


def _lanes_per_head(cols):
    return jnp.concatenate([jnp.broadcast_to(c, (CHUNK_ROWS, HEAD_W)) for c in cols], axis=1)


def _ret_log_decay(head):
    out = jnp.full(head.shape, math.log(1.0 - 2.0 ** -5.0), F32)
    for hh in range(1, N_HEADS):
        out = jnp.where(head == hh, math.log(1.0 - 2.0 ** (-5.0 - hh)), out)
    return out


def _layer_kernel(x_ref, cos_ref, sin_ref, s5re_in, s5im_in, hg_in, mc_in, mn_in, mm_in, rt_in,
                  nw_ref, w_ref, wo_ref, fnw_ref, are_ref, aim_ref, bcat_ref, ccat_ref, dsk_ref,
                  wglu_ref, hlb_ref, mlb_ref,
                  y_ref, s5re_out, s5im_out, hg_out, mc_out, mn_out, mm_out, rt_out,
                  h_s, pj_s, bu_s, mix_s, hre_s, him_s, sthg_s, stmc_s, strt_s, stn_s, stm_s,
                  *, NB, T, nvalid, final_norm):
    R = NB * T
    NCH = R // CHUNK_ROWS
    nseg = CHUNK_ROWS // T
    last = nvalid - 1
    padded = nvalid < T
    lgT = _log2(T)
    CR = CHUNK_ROWS
    j = pl.program_id(1)
    nj = pl.num_programs(1)

    bd_mask = (_iota((256, 256), 0) >> 6) == (_iota((256, 256), 1) >> 6)
    rt_mask = (_iota((RET_KW, 256), 0) >> 5) == (_iota((RET_KW, 256), 1) >> 6)
    hm256 = (_iota((STACK, 256), 0) >> 7) == (_iota((STACK, 256), 1) >> 6)
    hm128 = (_iota((STACK, RET_KW), 0) >> 7) == (_iota((STACK, RET_KW), 1) >> 5)
    bd_ones = bd_mask.astype(BF16)
    head_expand = hm256.astype(BF16)

    t_st = _iota((CR, STACK), 0)
    s_st = _iota((CR, STACK), 1) & (CR - 1)
    causal_st = ((t_st >> lgT) == (s_st >> lgT)) & (t_st >= s_st)
    diag_st = t_st == s_st
    t_sq = _iota((CR, CR), 0)
    s_sq = _iota((CR, CR), 1)
    causal_sq = ((t_sq >> lgT) == (s_sq >> lgT)) & (t_sq >= s_sq)
    triseg = causal_sq.astype(BF16)
    tl256 = _iota((CR, 256), 0) & (T - 1)
    tl128 = _iota((CR, 128), 0) & (T - 1)
    row256 = _iota((CR, 256), 0)
    valid256 = tl256 <= last
    valid128 = tl128 <= last

    @pl.when(j == 0)
    def _():
        hre_s[...] = s5re_in[...]
        him_s[...] = s5im_in[...]
        stn_s[...] = mn_in[...]
        stm_s[...] = mm_in[...]
        for b in range(NB):
            sthg_s[b] = _tile_state(hg_in[b], bd_mask)
            stmc_s[b] = _tile_state(mc_in[b], bd_mask)
            strt_s[b] = _tile_state(rt_in[b], rt_mask)

    x = x_ref[...].reshape(R, D_MODEL)
    ms = jnp.mean(x * x, axis=-1, keepdims=True)
    h_s[...] = (x * lax.rsqrt(ms + EPS) * nw_ref[...]).astype(BF16)

    def project(c0, width):
        for i in range(width // 256):
            pj_s[:, 256 * i:256 * (i + 1)] = _dot(h_s[...], w_ref[:, c0 + 256 * i:c0 + 256 * (i + 1)])

    def head_norm(o):
        mu = _dot_sel(o, bd_ones) * (1.0 / HEAD_W)
        d = o - mu
        var = _dot_sel(d * d, bd_ones) * (1.0 / HEAD_W)
        return d * lax.rsqrt(var + EPS)

    def for_chunks(fn):
        if NCH == 1:
            fn(0)
        else:
            def body(ch, carry):
                fn(ch)
                return carry
            lax.fori_loop(0, NCH, body, 0)

    def seg_last_rows(a):
        w = a.shape[1]
        pieces = [jnp.broadcast_to(a[sg * T + last:sg * T + last + 1, :], (T, w)) for sg in range(nseg)]
        return pieces[0] if nseg == 1 else jnp.concatenate(pieces, axis=0)

    def chunk_rows(ch):
        if isinstance(ch, int):
            return pl.ds(ch * CR, CR)
        return pl.ds(pl.multiple_of(ch * CR, CR), CR)

    project(C_S5, 512)
    u = pj_s[:, 0:256]
    ub = u.astype(BF16)
    for i in range(S5_TILES):
        drive = _dot(ub, bcat_ref[:, 256 * i:256 * (i + 1)])
        bu_s[2 * i] = drive[:, 0:128]
        bu_s[2 * i + 1] = drive[:, 128:256]
    a_re = are_ref[...]
    a_im = aim_ref[...]

    def s5_step(t, carry):
        idx = pl.ds(t, NB, stride=T)
        new = []
        for c in range(S5_TILES):
            hre, him = carry[2 * c], carry[2 * c + 1]
            ar = a_re[:, 128 * c:128 * (c + 1)]
            ai = a_im[:, 128 * c:128 * (c + 1)]
            nre = ar * hre - ai * him + bu_s[c, idx, :]
            nim = ar * him + ai * hre + bu_s[S5_TILES + c, idx, :]
            bu_s[c, idx, :] = nre
            bu_s[S5_TILES + c, idx, :] = nim
            new += [nre, nim]
        return tuple(new)

    init = []
    for c in range(S5_TILES):
        init += [hre_s[:, 128 * c:128 * (c + 1)], him_s[:, 128 * c:128 * (c + 1)]]
    fin = lax.fori_loop(0, nvalid, s5_step, tuple(init))
    for c in range(S5_TILES):
        hre_s[:, 128 * c:128 * (c + 1)] = fin[2 * c]
        him_s[:, 128 * c:128 * (c + 1)] = fin[2 * c + 1]
    hid = jnp.concatenate([bu_s[c].astype(BF16) for c in range(2 * S5_TILES)], axis=1)
    y = _dot(hid, ccat_ref[...]) + dsk_ref[...] * u
    y = _gelu_tanh(y)
    y = y * _sigmoid(_dot(y.astype(BF16), wglu_ref[...]))
    mix_s[:, 0:256] = (y * _silu(pj_s[:, 256:512])).astype(BF16)

    project(C_HG, 1024)
    log_lb = hlb_ref[0:1, :]
    log1m_lb = hlb_ref[1:2, :]
    one_m_lb = hlb_ref[2:3, :]

    def boundary_rows(c, m):
        blk = 2 * m
        if blk >= 8:
            c3 = c.reshape(CR // blk, blk, 256)
            return jnp.broadcast_to(c3[:, m - 1:m, :], (CR // blk, blk, 256)).reshape(CR, 256)
        c3 = c.reshape(CR // 8, 8, 256)
        sub = _iota((CR // 8, 8, 256), 1)
        out = None
        for jb in range(8 // blk):
            piece = jnp.broadcast_to(c3[:, jb * blk + m - 1:jb * blk + m, :], (CR // 8, 8, 256))
            out = piece if out is None else jnp.where((sub >> _log2(blk)) == jb, piece, out)
        return out.reshape(CR, 256)

    def hgrn_chunk(ch):
        rows = chunk_rows(ch)
        q = _silu(pj_s[rows, 0:256])
        xf = pj_s[rows, 256:512]
        v = pj_s[rows, 512:768].astype(BF16)
        gate = pj_s[rows, 768:1024]
        b_ = log1m_lb + _log_sigmoid(xf)
        log_f = jnp.maximum(log_lb, b_) + jnp.log1p(jnp.exp(-jnp.abs(log_lb - b_)))
        nlf = -log_f
        hk = one_m_lb * _sigmoid(-xf)
        if padded:
            nlf = jnp.where(valid256, nlf, 0.0)
            hk = jnp.where(valid256, hk, 0.0)
        c = _sel_dot(triseg, nlf)
        vs = _stack_heads(v, hm256)

        scores = jnp.where(diag_st, _dot_nt(q.astype(BF16), _stack_heads(hk.astype(BF16), hm256)), 0.0)
        m = T // 2
        while m >= 1:
            e = jnp.exp(-jnp.abs(c - boundary_rows(c, m)))
            second = (row256 & (2 * m - 1)) >= m
            qe = jnp.where(second, q * e, 0.0).astype(BF16)
            ke = jnp.where(second, 0.0, hk * e).astype(BF16)
            lg = _log2(2 * m)
            same_blk = (t_st >> lg) == (s_st >> lg)
            scores = scores + jnp.where(same_blk, _dot_nt(qe, _stack_heads(ke, hm256)), 0.0)
            m //= 2
        o = _dot(scores.astype(BF16), vs)

        qd = (q * jnp.exp(-c)).astype(BF16)
        kd = hk * jnp.exp(-(seg_last_rows(c) - c))
        if padded:
            kd = jnp.where(valid256, kd, 0.0)
        kd = kd.astype(BF16)
        o_inter = []
        for sg in range(nseg):
            b = ch * nseg + sg
            sl = slice(sg * T, (sg + 1) * T)
            st = sthg_s[b]
            o_inter.append(_dot_nt(qd[sl], st.astype(BF16)))
            c_last = c[sg * T + last:sg * T + last + 1, :]
            upd = _dot_tn(v[sl], kd[sl])
            sthg_s[b] = st * jnp.exp(-c_last) + jnp.where(bd_mask, upd, 0.0)
        o = o + (o_inter[0] if nseg == 1 else jnp.concatenate(o_inter, axis=0))
        mix_s[rows, 256:512] = (head_norm(o) * _silu(gate)).astype(BF16)

    for_chunks(hgrn_chunk)

    project(C_ML, 1280)
    pj_s[:, 1280:1408] = _dot(h_s[...], w_ref[:, C_GATE:C_GATE + 128])
    lane128 = _iota((CR, 128), 1)
    is_i = lane128 < N_HEADS
    is_f = (lane128 >= N_HEADS) & (lane128 < 2 * N_HEADS)

    def mlstm_chunk(ch):
        rows = chunk_rows(ch)
        q = pj_s[rows, 0:256]
        k = pj_s[rows, 256:512] * (HEAD_W ** -0.5)
        v = pj_s[rows, 512:768].astype(BF16)
        og = pj_s[rows, 768:1024]
        gate = pj_s[rows, 1024:1280]
        gpre = pj_s[rows, 1280:1408] + mlb_ref[...]
        log_i = gpre
        log_fg = _log_sigmoid(gpre)
        if padded:
            log_i = jnp.where(valid128, log_i, LOGI_PAD)
            log_fg = jnp.where(valid128, log_fg, 0.0)
        cum = _sel_dot(triseg, jnp.where(is_f, log_fg, 0.0))
        g = jnp.where(is_i, log_i, cum)
        g_t = jnp.transpose(g)

        if nseg == 1:
            ms_rows = jnp.broadcast_to(stm_s[pl.ds(ch, 1), :], (CR, 256))
        else:
            ms_rows = jnp.concatenate(
                [jnp.broadcast_to(stm_s[pl.ds(ch * nseg + sg, 1), :], (T, 256)) for sg in range(nseg)], axis=0)

        d_blocks, mt_cols, wp_cols, cf_cols, ig_cols = [], [], [], [], []
        for hh in range(N_HEADS):
            cf_col = g[:, N_HEADS + hh:N_HEADS + hh + 1]
            ig_col = g[:, hh:hh + 1]
            row_f = jnp.broadcast_to(g_t[N_HEADS + hh:N_HEADS + hh + 1, :], (CR, CR))
            row_i = jnp.broadcast_to(g_t[hh:hh + 1, :], (CR, CR))
            log_d = jnp.where(causal_sq, jnp.broadcast_to(cf_col, (CR, CR)) - row_f + row_i, -jnp.inf)
            w_prev = cf_col + ms_rows[:, hh * HEAD_W:hh * HEAD_W + 1]
            m_t = jnp.maximum(w_prev, jnp.max(log_d, axis=1, keepdims=True))
            d_blocks.append(jnp.exp(log_d - m_t))
            mt_cols.append(m_t)
            wp_cols.append(w_prev)
            cf_cols.append(cf_col)
            ig_cols.append(ig_col)
        d_mat = jnp.concatenate(d_blocks, axis=1)
        mt = _lanes_per_head(mt_cols)
        s_prev = jnp.exp(_lanes_per_head(wp_cols) - mt)
        cumf = _lanes_per_head(cf_cols)
        lig = _lanes_per_head(ig_cols)

        qb = q.astype(BF16)
        scores = _dot_nt(qb, _stack_heads(k.astype(BF16), hm256)) * d_mat
        num = _dot(scores.astype(BF16), _stack_heads(v, hm256))
        den = _dot_sel(scores, head_expand)

        w_k = jnp.exp(seg_last_rows(cumf) - cumf + lig - seg_last_rows(mt))
        if padded:
            w_k = jnp.where(valid256, w_k, 0.0)
        kw = w_k * k
        kwb = kw.astype(BF16)
        qc, qn = [], []
        for sg in range(nseg):
            b = ch * nseg + sg
            sl = slice(sg * T, (sg + 1) * T)
            cs = stmc_s[b]
            ns = stn_s[pl.ds(b, 1), :]
            ms_b = stm_s[pl.ds(b, 1), :]
            qc.append(_dot(qb[sl], cs.astype(BF16)))
            qn.append(_dot_sel(q[sl] * ns, bd_ones))
            lr = sg * T + last
            m_new = mt[lr:lr + 1, :]
            carry = jnp.exp(cumf[lr:lr + 1, :] + ms_b - m_new)
            stmc_s[b] = carry * cs + jnp.where(bd_mask, _dot_tn(kwb[sl], v[sl]), 0.0)
            stn_s[pl.ds(b, 1), :] = carry * ns + jnp.sum(kw[sl], axis=0, keepdims=True)
            stm_s[pl.ds(b, 1), :] = m_new
        qc = qc[0] if nseg == 1 else jnp.concatenate(qc, axis=0)
        qn = qn[0] if nseg == 1 else jnp.concatenate(qn, axis=0)
        num = s_prev * qc + num
        den = s_prev * qn + den
        hcell = num / jnp.maximum(jnp.abs(den), jnp.exp(-mt))
        mix_s[rows, 512:768] = (head_norm(_sigmoid(og) * hcell) * _silu(gate)).astype(BF16)

    for_chunks(mlstm_chunk)

    project(C_RT, 768)
    gam_st = _ret_log_decay(_iota((CR, STACK), 1) >> 7)
    d_ret = jnp.where(causal_st, jnp.exp(((t_st - s_st) & (T - 1)).astype(F32) * gam_st), 0.0)
    gam128 = _ret_log_decay(_iota((CR, 128), 1) >> 5)
    gam256 = _ret_log_decay(_iota((1, 256), 1) >> 6)
    q_dec = jnp.exp((tl128 + 1).astype(F32) * gam128)
    k_dec = jnp.where(valid128, jnp.exp((last - tl128).astype(F32) * gam128), 0.0)
    s_dec = jnp.exp(float(last + 1) * gam256)
    first_half = (_iota((CR, 128), 1) & (RET_DK - 1)) < (RET_DK // 2)

    def rotary(t, cos, sin_signed):
        partner = jnp.where(first_half, pltpu.roll(t, 128 - RET_DK // 2, axis=1), pltpu.roll(t, RET_DK // 2, axis=1))
        return t * cos + partner * sin_signed

    def ret_chunk(ch):
        rows = chunk_rows(ch)
        cos = jnp.concatenate([cos_ref[...]] * nseg, axis=0)
        sin_signed = jnp.concatenate([sin_ref[...]] * nseg, axis=0)
        q = rotary(pj_s[rows, 0:128], cos, sin_signed)
        k = rotary(pj_s[rows, 128:256], cos, sin_signed) * (RET_DK ** -0.5)
        v = pj_s[rows, 256:512].astype(BF16)
        gate = pj_s[rows, 512:768]
        scores = _dot_nt(q.astype(BF16), _stack_heads(k.astype(BF16), hm128)) * d_ret
        o = _dot(scores.astype(BF16), _stack_heads(v, hm256))
        qd = (q * q_dec).astype(BF16)
        kd = (k * k_dec).astype(BF16)
        o_inter = []
        for sg in range(nseg):
            b = ch * nseg + sg
            sl = slice(sg * T, (sg + 1) * T)
            st = strt_s[b]
            o_inter.append(_dot(qd[sl], st.astype(BF16)))
            strt_s[b] = st * s_dec + jnp.where(rt_mask, _dot_tn(kd[sl], v[sl]), 0.0)
        o = o + (o_inter[0] if nseg == 1 else jnp.concatenate(o_inter, axis=0))
        mix_s[rows, 768:1024] = (head_norm(o) * _silu(gate)).astype(BF16)

    for_chunks(ret_chunk)

    out = x_ref[...].reshape(R, D_MODEL) + _dot(mix_s[...], wo_ref[...])
    if final_norm:
        ms2 = jnp.mean(out * out, axis=-1, keepdims=True)
        out = out * lax.rsqrt(ms2 + EPS) * fnw_ref[...]
    y_ref[...] = out.reshape(NB, T, D_MODEL)

    @pl.when(j == nj - 1)
    def _():
        s5re_out[...] = hre_s[...]
        s5im_out[...] = him_s[...]
        mn_out[...] = stn_s[...]
        mm_out[...] = stm_s[...]
        for b in range(NB):
            hg_out[b] = _untile_state(sthg_s[b])
            mc_out[b] = _untile_state(stmc_s[b])
            rt_out[b] = _untile_state(strt_s[b])


def _run_layer(x, cos_tab, sin_tab, states, weights, *, NB, T, nvalid, final_norm):
    B, L, _ = x.shape
    nbb, ntc = B // NB, L // T
    R = NB * T
    s5re, s5im, hg, mc, mn, mm, rt = states

    def bspec(shape):
        nd = len(shape)
        return pl.BlockSpec((NB,) + shape[1:], lambda i, j, nd=nd: (i,) + (0,) * (nd - 1))

    def wspec(a):
        nd = a.ndim
        return pl.BlockSpec(a.shape, lambda i, j, nd=nd: (0,) * nd, pipeline_mode=pl.Buffered(1))

    in_specs = [
        pl.BlockSpec((NB, T, D_MODEL), lambda i, j: (i, j, 0)),
        pl.BlockSpec((T, 128), lambda i, j: (j, 0)),
        pl.BlockSpec((T, 128), lambda i, j: (j, 0)),
    ] + [bspec(s.shape) for s in states] + [wspec(w) for w in weights]
    out_shape = [jax.ShapeDtypeStruct(x.shape, F32)] + [jax.ShapeDtypeStruct(s.shape, F32) for s in states]
    out_specs = [pl.BlockSpec((NB, T, D_MODEL), lambda i, j: (i, j, 0))] + [bspec(s.shape) for s in states]
    scratch = [
        pltpu.VMEM((R, D_MODEL), BF16),
        pltpu.VMEM((R, PJ_COLS), F32),
        pltpu.VMEM((2 * S5_TILES, R, 128), F32),
        pltpu.VMEM((R, D_MODEL), BF16),
        pltpu.VMEM((NB, S5_N), F32),
        pltpu.VMEM((NB, S5_N), F32),
        pltpu.VMEM((NB, 256, 256), F32),
        pltpu.VMEM((NB, 256, 256), F32),
        pltpu.VMEM((NB, RET_KW, 256), F32),
        pltpu.VMEM((NB, 256), F32),
        pltpu.VMEM((NB, 256), F32),
    ]
    kern = functools.partial(_layer_kernel, NB=NB, T=T, nvalid=nvalid, final_norm=final_norm)
    outs = pl.pallas_call(
        kern,
        grid=(nbb, ntc),
        in_specs=in_specs,
        out_specs=out_specs,
        out_shape=out_shape,
        scratch_shapes=scratch,
        compiler_params=pltpu.CompilerParams(
            dimension_semantics=("arbitrary", "arbitrary"),
            vmem_limit_bytes=VMEM_LIMIT_BYTES,
        ),
    )(x, cos_tab, sin_tab, *states, *weights)
    return outs[0], outs[1:]


def _rotary_tables(pos):
    half = RET_DK // 2
    inv_freq = ROPE_BASE ** (-jnp.arange(half, dtype=F32) / half)
    ang = pos.astype(F32)[:, None] * inv_freq[None, :]
    cos, sin = jnp.cos(ang), jnp.sin(ang)
    cos_t = jnp.tile(cos, (1, 128 // half))
    sin_t = jnp.tile(jnp.concatenate([-sin, sin], axis=1), (1, 128 // RET_DK))
    return cos_t, sin_t


def _block_diag(blocks):
    g, a, b = blocks.shape
    eye = jnp.eye(g, dtype=blocks.dtype)
    return jnp.einsum('gab,gh->gahb', blocks, eye).reshape(g * a, g * b)


def _layer_weights(l, norm_w, w_in, w_out, final_norm_w, lam_re, lam_im, b_re, b_im, c_re, c_im,
                   s5_d, s5_log_dt, s5_w_glu, hgrn_lb_logits, mlstm_b_i, mlstm_b_f):
    wl = w_in[l]
    w_all = jnp.concatenate(
        [wl[:, 0:2560], wl[:, 2568:3592], wl[:, 2560:2568], jnp.zeros((D_MODEL, 120), wl.dtype)], axis=1
    ).astype(BF16)
    lam = lax.complex(lam_re[l].astype(F32), lam_im[l].astype(F32))
    dt = jnp.exp(s5_log_dt[l].astype(F32))[:, None]
    a_bar = jnp.exp(lam * dt)
    b_bar = ((a_bar - 1.0) / lam)[:, :, None] * lax.complex(b_re[l].astype(F32), b_im[l].astype(F32))
    bt = jnp.swapaxes(b_bar, 1, 2)
    bcat = jnp.concatenate([_block_diag(jnp.real(bt)), _block_diag(jnp.imag(bt))], axis=1).astype(BF16)
    ct_re = jnp.swapaxes(c_re[l].astype(F32), 1, 2)
    ct_im = jnp.swapaxes(c_im[l].astype(F32), 1, 2)
    ccat = jnp.concatenate([_block_diag(ct_re), -_block_diag(ct_im)], axis=0).astype(BF16)
    lb = jnp.cumsum(jax.nn.softmax(hgrn_lb_logits.astype(F32), axis=0), axis=0)
    lb = (lb - lb[0])[l]
    hlb = jnp.stack([jnp.log(lb), jnp.log1p(-lb), 1.0 - lb] + [jnp.zeros_like(lb)] * 5, axis=0)
    mlb = jnp.concatenate([mlstm_b_i[l].astype(F32), mlstm_b_f[l].astype(F32), jnp.zeros((120,), F32)])[None, :]
    return (
        norm_w[l].astype(F32)[None, :], w_all, w_out[l].astype(BF16), final_norm_w.astype(F32)[None, :],
        jnp.real(a_bar).reshape(1, S5_N), jnp.imag(a_bar).reshape(1, S5_N), bcat, ccat,
        s5_d[l].astype(F32)[None, :], s5_w_glu[l].astype(BF16), hlb, mlb,
    )


def _to_kernel_states(s5re, s5im, hg, mc, mn, mm, rt):
    b = s5re.shape[0]
    return (
        s5re.reshape(b, S5_N), s5im.reshape(b, S5_N),
        jnp.swapaxes(hg, 2, 3).reshape(b, 256, HEAD_W),
        mc.reshape(b, 256, HEAD_W), mn.reshape(b, 256), jnp.repeat(mm, HEAD_W, axis=1),
        rt.reshape(b, RET_KW, HEAD_W),
    )


def _from_kernel_states(states):
    s5re, s5im, hg, mc, mn, mm, rt = states
    b = s5re.shape[0]
    return (
        s5re.reshape(b, S5_GROUPS, S5_STATE), s5im.reshape(b, S5_GROUPS, S5_STATE),
        jnp.swapaxes(hg.reshape(b, N_HEADS, HEAD_W, HEAD_W), 2, 3),
        mc.reshape(b, N_HEADS, HEAD_W, HEAD_W), mn.reshape(b, N_HEADS, HEAD_W), mm[:, ::HEAD_W],
        rt.reshape(b, N_HEADS, RET_DK, HEAD_W),
    )


def _trunk(x, cos_tab, sin_tab, states, params, *, NB, T, nvalid):
    new = [[] for _ in range(7)]
    for l in range(DEPTH):
        weights = _layer_weights(l, *params)
        st_l = _to_kernel_states(*[s[l] for s in states])
        x, st_new = _run_layer(x, cos_tab, sin_tab, st_l, weights, NB=NB, T=T, nvalid=nvalid,
                               final_norm=(l == DEPTH - 1))
        for lst, val in zip(new, _from_kernel_states(st_new)):
            lst.append(val)
    return x, [jnp.stack(v, axis=0) for v in new]


def kernel(x_prompt, x_sample, state_s5_re, state_s5_im, state_hgrn, state_mlstm_C, state_mlstm_n,
           state_mlstm_m, state_ret, norm_w, w_in, w_out, final_norm_w, s5_lambda_re, s5_lambda_im,
           s5_B_re, s5_B_im, s5_C_re, s5_C_im, s5_D, s5_log_dt, s5_w_glu, hgrn_lb_logits,
           mlstm_b_i, mlstm_b_f):
    params = (norm_w, w_in, w_out, final_norm_w, s5_lambda_re, s5_lambda_im, s5_B_re, s5_B_im,
              s5_C_re, s5_C_im, s5_D, s5_log_dt, s5_w_glu, hgrn_lb_logits, mlstm_b_i, mlstm_b_f)
    bp, lp = x_prompt.shape[0], x_prompt.shape[1]
    bs, ls = x_sample.shape[0], x_sample.shape[1]

    zero_states = (
        jnp.zeros((DEPTH, bp, S5_GROUPS, S5_STATE), F32), jnp.zeros((DEPTH, bp, S5_GROUPS, S5_STATE), F32),
        jnp.zeros((DEPTH, bp, N_HEADS, HEAD_W, HEAD_W), F32), jnp.zeros((DEPTH, bp, N_HEADS, HEAD_W, HEAD_W), F32),
        jnp.zeros((DEPTH, bp, N_HEADS, HEAD_W), F32), jnp.zeros((DEPTH, bp, N_HEADS), F32),
        jnp.zeros((DEPTH, bp, N_HEADS, RET_DK, HEAD_W), F32),
    )
    cos_p, sin_p = _rotary_tables(jnp.arange(lp, dtype=jnp.int32))
    y_prompt, p_states = _trunk(x_prompt, cos_p, sin_p, zero_states, params, NB=bp, T=PROMPT_T, nvalid=PROMPT_T)

    t_pad = 8
    nb_s = CHUNK_ROWS // t_pad
    x_s = jnp.pad(x_sample, ((0, 0), (0, t_pad - ls), (0, 0)))
    cos_s, sin_s = _rotary_tables(PAST_LEN + jnp.arange(t_pad, dtype=jnp.int32))
    sample_states = (state_s5_re, state_s5_im, state_hgrn, state_mlstm_C, state_mlstm_n, state_mlstm_m, state_ret)
    y_s, d_states = _trunk(x_s, cos_s, sin_s, sample_states, params, NB=nb_s, T=t_pad, nvalid=ls)
    y_sample = y_s[:, :ls]
    return (y_prompt, y_sample, *p_states, *d_states)
```

```python
import functools
import math

import numpy as np
import jax
import jax.numpy as jnp
from jax import lax
from jax.experimental import pallas as pl
from jax.experimental.pallas import tpu as pltpu

F32 = jnp.float32
BF16 = jnp.bfloat16

D_MODEL = 1024
DEPTH = 4
BRANCH_W = 256
N_HEADS = 4
HEAD_W = 64
S5_GROUPS = 16
S5_CH = 16
S5_STATE = 64
S5_N = S5_GROUPS * S5_STATE
S5_TILES = S5_N // 128
RET_DK = 32
RET_KW = N_HEADS * RET_DK
ROPE_BASE = 10000.0
EPS = 1e-6
LOGI_PAD = -1e30
PAST_LEN = 16384

CHUNK_ROWS = 128
STACK = N_HEADS * CHUNK_ROWS
PROMPT_T = 64
VMEM_LIMIT_BYTES = 60 * 1024 * 1024

C_S5 = 0
C_HG = 512
C_ML = 1536
C_RT = 2816
C_GATE = 3584
W_ALL_COLS = 3712
PJ_COLS = 1408


def _dot(a, b):
    return jnp.dot(a, b, preferred_element_type=F32)


def _dot_nt(a, b):
    return lax.dot_general(a, b, (((1,), (1,)), ((), ())), preferred_element_type=F32)


def _dot_tn(a, b):
    return lax.dot_general(a, b, (((0,), (0,)), ((), ())), preferred_element_type=F32)


def _split3(x):
    hi = x.astype(BF16)
    r = x - hi.astype(F32)
    mid = r.astype(BF16)
    lo = (r - mid.astype(F32)).astype(BF16)
    return hi, mid, lo


def _sel_dot(sel_bf, x):
    hi, mid, lo = _split3(x)
    return _dot(sel_bf, hi) + _dot(sel_bf, mid) + _dot(sel_bf, lo)


def _dot_sel(x, sel_bf):
    hi = x.astype(BF16)
    lo = (x - hi.astype(F32)).astype(BF16)
    return _dot(hi, sel_bf) + _dot(lo, sel_bf)


def _sigmoid(x):
    return 0.5 + 0.5 * jnp.tanh(0.5 * x)


def _silu(x):
    return x * _sigmoid(x)


def _softplus_neg_abs(x):
    return jnp.log(1.0 + jnp.exp(-jnp.abs(x)))


def _log_sigmoid(x):
    return jnp.minimum(x, 0.0) - _softplus_neg_abs(x)


def _gelu_tanh(x):
    return 0.5 * x * (1.0 + jnp.tanh(math.sqrt(2.0 / math.pi) * (x + 0.044715 * (x * x * x))))


def _iota(shape, dim):
    return lax.broadcasted_iota(jnp.int32, shape, dim)


def _log2(n):
    k = int(round(math.log2(n)))
    assert (1 << k) == n, n
    return k


def _stack_heads(x_bf, head_mask):
    return jnp.where(head_mask, jnp.concatenate([x_bf] * N_HEADS, axis=0), jnp.zeros((), x_bf.dtype))


def _tile_state(s, mask):
    return jnp.where(mask, jnp.concatenate([s] * N_HEADS, axis=1), 0.0)


def _untile_state(s_bd):
    return (s_bd[:, 0:64] + s_bd[:, 64:128]) + (s_bd[:, 128:192] + s_bd[:, 192:256])


def _lanes_per_head(cols):
    return jnp.concatenate([jnp.broadcast_to(c, (CHUNK_ROWS, HEAD_W)) for c in cols], axis=1)


def _ret_log_decay(head):
    out = jnp.full(head.shape, math.log(1.0 - 2.0 ** -5.0), F32)
    for hh in range(1, N_HEADS):
        out = jnp.where(head == hh, math.log(1.0 - 2.0 ** (-5.0 - hh)), out)
    return out


def _layer_kernel(x_ref, cos_ref, sin_ref, s5re_in, s5im_in, hg_in, mc_in, mn_in, mm_in, rt_in,
                  nw_ref, w_ref, wo_ref, fnw_ref, are_ref, aim_ref, bcat_ref, ccat_ref, dsk_ref,
                  wglu_ref, hlb_ref, mlb_ref,
                  y_ref, s5re_out, s5im_out, hg_out, mc_out, mn_out, mm_out, rt_out,
                  h_s, pj_s, bu_s, mix_s, hre_s, him_s, sthg_s, stmc_s, strt_s, stn_s, stm_s, perm_s, permt_s,
                  *, NB, T, nvalid, final_norm):
    R = NB * T
    NCH = R // CHUNK_ROWS
    nseg = CHUNK_ROWS // T
    last = nvalid - 1
    padded = nvalid < T
    lgT = _log2(T)
    lgNB = _log2(NB)
    CR = CHUNK_ROWS
    j = pl.program_id(1)
    nj = pl.num_programs(1)

    bd_mask = (_iota((256, 256), 0) >> 6) == (_iota((256, 256), 1) >> 6)
    rt_mask = (_iota((RET_KW, 256), 0) >> 5) == (_iota((RET_KW, 256), 1) >> 6)
    hm256 = (_iota((STACK, 256), 0) >> 7) == (_iota((STACK, 256), 1) >> 6)
    hm128 = (_iota((STACK, RET_KW), 0) >> 7) == (_iota((STACK, RET_KW), 1) >> 5)
    bd_ones = bd_mask.astype(BF16)
    head_expand = hm256.astype(BF16)

    t_st = _iota((CR, STACK), 0)
    s_st = _iota((CR, STACK), 1) & (CR - 1)
    causal_st = ((t_st >> lgT) == (s_st >> lgT)) & (t_st >= s_st)
    diag_st = t_st == s_st
    t_sq = _iota((CR, CR), 0)
    s_sq = _iota((CR, CR), 1)
    causal_sq = ((t_sq >> lgT) == (s_sq >> lgT)) & (t_sq >= s_sq)
    triseg = causal_sq.astype(BF16)
    tl256 = _iota((CR, 256), 0) & (T - 1)
    tl128 = _iota((CR, 128), 0) & (T - 1)
    row256 = _iota((CR, 256), 0)
    valid256 = tl256 <= last
    valid128 = tl128 <= last

    @pl.when(j == 0)
    def _():
        ri = _iota((R, R), 0)
        ci = _iota((R, R), 1)
        perm_s[...] = (ci == (((ri & (NB - 1)) << lgT) + (ri >> lgNB))).astype(BF16)
        permt_s[...] = (ri == (((ci & (NB - 1)) << lgT) + (ci >> lgNB))).astype(BF16)
        hre_s[...] = s5re_in[...]
        him_s[...] = s5im_in[...]
        stn_s[...] = mn_in[...]
        stm_s[...] = mm_in[...]
        for b in range(NB):
            sthg_s[b] = _tile_state(hg_in[b], bd_mask)
            stmc_s[b] = _tile_state(mc_in[b], bd_mask)
            strt_s[b] = _tile_state(rt_in[b], rt_mask)

    x = x_ref[...].reshape(R, D_MODEL)
    ms = jnp.mean(x * x, axis=-1, keepdims=True)
    h_s[...] = (x * lax.rsqrt(ms + EPS) * nw_ref[...]).astype(BF16)

    def project(c0, width):
        for i in range(width // 256):
            pj_s[:, 256 * i:256 * (i + 1)] = _dot(h_s[...], w_ref[:, c0 + 256 * i:c0 + 256 * (i + 1)])

    def head_norm(o):
        mu = _dot_sel(o, bd_ones) * (1.0 / HEAD_W)
        d = o - mu
        var = _dot_sel(d * d, bd_ones) * (1.0 / HEAD_W)
        return d * lax.rsqrt(var + EPS)

    def for_chunks(fn):
        if NCH == 1:
            fn(0)
        else:
            def body(ch, carry):
                fn(ch)
                return carry
            lax.fori_loop(0, NCH, body, 0)

    def seg_last_rows(a):
        w = a.shape[1]
        pieces = [jnp.broadcast_to(a[sg * T + last:sg * T + last + 1, :], (T, w)) for sg in range(nseg)]
        return pieces[0] if nseg == 1 else jnp.concatenate(pieces, axis=0)

    def chunk_rows(ch):
        if isinstance(ch, int):
            return pl.ds(ch * CR, CR)
        return pl.ds(pl.multiple_of(ch * CR, CR), CR)

    project(C_S5, 512)
    u = pj_s[:, 0:256]
    ub = _dot(perm_s[...], u.astype(BF16)).astype(BF16)
    for i in range(S5_TILES):
        drive = _dot(ub, bcat_ref[:, 256 * i:256 * (i + 1)])
        bu_s[2 * i] = drive[:, 0:128]
        bu_s[2 * i + 1] = drive[:, 128:256]
    a_re = are_ref[...]
    a_im = aim_ref[...]

    def s5_step(t, carry):
        idx = pl.ds(pl.multiple_of(t * NB, NB), NB)
        new = []
        for c in range(S5_TILES):
            hre, him = carry[2 * c], carry[2 * c + 1]
            ar = a_re[:, 128 * c:128 * (c + 1)]
            ai = a_im[:, 128 * c:128 * (c + 1)]
            nre = ar * hre - ai * him + bu_s[c, idx, :]
            nim = ar * him + ai * hre + bu_s[S5_TILES + c, idx, :]
            bu_s[c, idx, :] = nre
            bu_s[S5_TILES + c, idx, :] = nim
            new += [nre, nim]
        return tuple(new)

    init = []
    for c in range(S5_TILES):
        init += [hre_s[:, 128 * c:128 * (c + 1)], him_s[:, 128 * c:128 * (c + 1)]]
    fin = lax.fori_loop(0, nvalid, s5_step, tuple(init))
    for c in range(S5_TILES):
        hre_s[:, 128 * c:128 * (c + 1)] = fin[2 * c]
        him_s[:, 128 * c:128 * (c + 1)] = fin[2 * c + 1]
    hid = jnp.concatenate([bu_s[c].astype(BF16) for c in range(2 * S5_TILES)], axis=1)
    y = _sel_dot(permt_s[...], _dot(hid, ccat_ref[...])) + dsk_ref[...] * u
    y = _gelu_tanh(y)
    y = y * _sigmoid(_dot(y.astype(BF16), wglu_ref[...]))
    mix_s[:, 0:256] = (y * _silu(pj_s[:, 256:512])).astype(BF16)

    project(C_HG, 1024)
    log_lb = hlb_ref[0:1, :]
    log1m_lb = hlb_ref[1:2, :]
    one_m_lb = hlb_ref[2:3, :]

    def boundary_rows(c, m):
        blk = 2 * m
        if blk >= 8:
            c3 = c.reshape(CR // blk, blk, 256)
            return jnp.broadcast_to(c3[:, m - 1:m, :], (CR // blk, blk, 256)).reshape(CR, 256)
        c3 = c.reshape(CR // 8, 8, 256)
        sub = _iota((CR // 8, 8, 256), 1)
        out = None
        for jb in range(8 // blk):
            piece = jnp.broadcast_to(c3[:, jb * blk + m - 1:jb * blk + m, :], (CR // 8, 8, 256))
            out = piece if out is None else jnp.where((sub >> _log2(blk)) == jb, piece, out)
        return out.reshape(CR, 256)

    def hgrn_chunk(ch):
        rows = chunk_rows(ch)
        q = _silu(pj_s[rows, 0:256])
        xf = pj_s[rows, 256:512]
        v = pj_s[rows, 512:768].astype(BF16)
        gate = pj_s[rows, 768:1024]
        b_ = log1m_lb + _log_sigmoid(xf)
        log_f = jnp.maximum(log_lb, b_) + _softplus_neg_abs(log_lb - b_)
        nlf = -log_f
        hk = one_m_lb * _sigmoid(-xf)
        if padded:
            nlf = jnp.where(valid256, nlf, 0.0)
            hk = jnp.where(valid256, hk, 0.0)
        c = _sel_dot(triseg, nlf)
        vs = _stack_heads(v, hm256)

        scores = jnp.where(diag_st, _dot_nt(q.astype(BF16), _stack_heads(hk.astype(BF16), hm256)), 0.0)
        m = T // 2
        while m >= 1:
            e = jnp.exp(-jnp.abs(c - boundary_rows(c, m)))
            second = (row256 & (2 * m - 1)) >= m
            qe = jnp.where(second, q * e, 0.0).astype(BF16)
            ke = jnp.where(second, 0.0, hk * e).astype(BF16)
            lg = _log2(2 * m)
            same_blk = (t_st >> lg) == (s_st >> lg)
            scores = scores + jnp.where(same_blk, _dot_nt(qe, _stack_heads(ke, hm256)), 0.0)
            m //= 2
        o = _dot(scores.astype(BF16), vs)

        qd = (q * jnp.exp(-c)).astype(BF16)
        kd = hk * jnp.exp(-(seg_last_rows(c) - c))
        if padded:
            kd = jnp.where(valid256, kd, 0.0)
        kd = kd.astype(BF16)
        o_inter = []
        for sg in range(nseg):
            b = ch * nseg + sg
            sl = slice(sg * T, (sg + 1) * T)
            st = sthg_s[b]
            o_inter.append(_dot_nt(qd[sl], st.astype(BF16)))
            c_last = c[sg * T + last:sg * T + last + 1, :]
            upd = _dot_tn(v[sl], kd[sl])
            sthg_s[b] = st * jnp.exp(-c_last) + jnp.where(bd_mask, upd, 0.0)
        o = o + (o_inter[0] if nseg == 1 else jnp.concatenate(o_inter, axis=0))
        mix_s[rows, 256:512] = (head_norm(o) * _silu(gate)).astype(BF16)

    for_chunks(hgrn_chunk)

    project(C_ML, 1280)
    pj_s[:, 1280:1408] = _dot(h_s[...], w_ref[:, C_GATE:C_GATE + 128])
    lane128 = _iota((CR, 128), 1)
    is_i = lane128 < N_HEADS
    is_f = (lane128 >= N_HEADS) & (lane128 < 2 * N_HEADS)

    def mlstm_chunk(ch):
        rows = chunk_rows(ch)
        q = pj_s[rows, 0:256]
        k = pj_s[rows, 256:512] * (HEAD_W ** -0.5)
        v = pj_s[rows, 512:768].astype(BF16)
        og = pj_s[rows, 768:1024]
        gate = pj_s[rows, 1024:1280]
        gpre = pj_s[rows, 1280:1408] + mlb_ref[...]
        log_i = gpre
        log_fg = _log_sigmoid(gpre)
        if padded:
            log_i = jnp.where(valid128, log_i, LOGI_PAD)
            log_fg = jnp.where(valid128, log_fg, 0.0)
        cum = _sel_dot(triseg, jnp.where(is_f, log_fg, 0.0))
        g = jnp.where(is_i, log_i, cum)
        g_t = jnp.transpose(g)

        if nseg == 1:
            ms_rows = jnp.broadcast_to(stm_s[pl.ds(ch, 1), :], (CR, 256))
        else:
            ms_rows = jnp.concatenate(
                [jnp.broadcast_to(stm_s[pl.ds(ch * nseg + sg, 1), :], (T, 256)) for sg in range(nseg)], axis=0)

        d_blocks, mt_cols, wp_cols, cf_cols, ig_cols = [], [], [], [], []
        for hh in range(N_HEADS):
            cf_col = g[:, N_HEADS + hh:N_HEADS + hh + 1]
            ig_col = g[:, hh:hh + 1]
            row_f = jnp.broadcast_to(g_t[N_HEADS + hh:N_HEADS + hh + 1, :], (CR, CR))
            row_i = jnp.broadcast_to(g_t[hh:hh + 1, :], (CR, CR))
            log_d = jnp.where(causal_sq, jnp.broadcast_to(cf_col, (CR, CR)) - row_f + row_i, -jnp.inf)
            w_prev = cf_col + ms_rows[:, hh * HEAD_W:hh * HEAD_W + 1]
            m_t = jnp.maximum(w_prev, jnp.max(log_d, axis=1, keepdims=True))
            d_blocks.append(jnp.exp(log_d - m_t))
            mt_cols.append(m_t)
            wp_cols.append(w_prev)
            cf_cols.append(cf_col)
            ig_cols.append(ig_col)
        d_mat = jnp.concatenate(d_blocks, axis=1)
        mt = _lanes_per_head(mt_cols)
        s_prev = jnp.exp(_lanes_per_head(wp_cols) - mt)
        cumf = _lanes_per_head(cf_cols)
        lig = _lanes_per_head(ig_cols)

        qb = q.astype(BF16)
        scores = _dot_nt(qb, _stack_heads(k.astype(BF16), hm256)) * d_mat
        num = _dot(scores.astype(BF16), _stack_heads(v, hm256))
        den = _dot_sel(scores, head_expand)

        w_k = jnp.exp(seg_last_rows(cumf) - cumf + lig - seg_last_rows(mt))
        if padded:
            w_k = jnp.where(valid256, w_k, 0.0)
        kw = w_k * k
        kwb = kw.astype(BF16)
        qc, qn = [], []
        for sg in range(nseg):
            b = ch * nseg + sg
            sl = slice(sg * T, (sg + 1) * T)
            cs = stmc_s[b]
            ns = stn_s[pl.ds(b, 1), :]
            ms_b = stm_s[pl.ds(b, 1), :]
            qc.append(_dot(qb[sl], cs.astype(BF16)))
            qn.append(_dot_sel(q[sl] * ns, bd_ones))
            lr = sg * T + last
            m_new = mt[lr:lr + 1, :]
            carry = jnp.exp(cumf[lr:lr + 1, :] + ms_b - m_new)
            stmc_s[b] = carry * cs + jnp.where(bd_mask, _dot_tn(kwb[sl], v[sl]), 0.0)
            stn_s[pl.ds(b, 1), :] = carry * ns + jnp.sum(kw[sl], axis=0, keepdims=True)
            stm_s[pl.ds(b, 1), :] = m_new
        qc = qc[0] if nseg == 1 else jnp.concatenate(qc, axis=0)
        qn = qn[0] if nseg == 1 else jnp.concatenate(qn, axis=0)
        num = s_prev * qc + num
        den = s_prev * qn + den
        hcell = num / jnp.maximum(jnp.abs(den), jnp.exp(-mt))
        mix_s[rows, 512:768] = (head_norm(_sigmoid(og) * hcell) * _silu(gate)).astype(BF16)

    for_chunks(mlstm_chunk)

    project(C_RT, 768)
    gam_st = _ret_log_decay(_iota((CR, STACK), 1) >> 7)
    d_ret = jnp.where(causal_st, jnp.exp(((t_st - s_st) & (T - 1)).astype(F32) * gam_st), 0.0)
    gam128 = _ret_log_decay(_iota((CR, 128), 1) >> 5)
    gam256 = _ret_log_decay(_iota((1, 256), 1) >> 6)
    q_dec = jnp.exp((tl128 + 1).astype(F32) * gam128)
    k_dec = jnp.where(valid128, jnp.exp((last - tl128).astype(F32) * gam128), 0.0)
    s_dec = jnp.exp(float(last + 1) * gam256)
    first_half = (_iota((CR, 128), 1) & (RET_DK - 1)) < (RET_DK // 2)

    def rotary(t, cos, sin_signed):
        partner = jnp.where(first_half, pltpu.roll(t, 128 - RET_DK // 2, axis=1), pltpu.roll(t, RET_DK // 2, axis=1))
        return t * cos + partner * sin_signed

    def ret_chunk(ch):
        rows = chunk_rows(ch)
        cos = jnp.concatenate([cos_ref[...]] * nseg, axis=0)
        sin_signed = jnp.concatenate([sin_ref[...]] * nseg, axis=0)
        q = rotary(pj_s[rows, 0:128], cos, sin_signed)
        k = rotary(pj_s[rows, 128:256], cos, sin_signed) * (RET_DK ** -0.5)
        v = pj_s[rows, 256:512].astype(BF16)
        gate = pj_s[rows, 512:768]
        scores = _dot_nt(q.astype(BF16), _stack_heads(k.astype(BF16), hm128)) * d_ret
        o = _dot(scores.astype(BF16), _stack_heads(v, hm256))
        qd = (q * q_dec).astype(BF16)
        kd = (k * k_dec).astype(BF16)
        o_inter = []
        for sg in range(nseg):
            b = ch * nseg + sg
            sl = slice(sg * T, (sg + 1) * T)
            st = strt_s[b]
            o_inter.append(_dot(qd[sl], st.astype(BF16)))
            strt_s[b] = st * s_dec + jnp.where(rt_mask, _dot_tn(kd[sl], v[sl]), 0.0)
        o = o + (o_inter[0] if nseg == 1 else jnp.concatenate(o_inter, axis=0))
        mix_s[rows, 768:1024] = (head_norm(o) * _silu(gate)).astype(BF16)

    for_chunks(ret_chunk)

    out = x_ref[...].reshape(R, D_MODEL) + _dot(mix_s[...], wo_ref[...])
    if final_norm:
        ms2 = jnp.mean(out * out, axis=-1, keepdims=True)
        out = out * lax.rsqrt(ms2 + EPS) * fnw_ref[...]
    y_ref[...] = out.reshape(NB, T, D_MODEL)

    @pl.when(j == nj - 1)
    def _():
        s5re_out[...] = hre_s[...]
        s5im_out[...] = him_s[...]
        mn_out[...] = stn_s[...]
        mm_out[...] = stm_s[...]
        for b in range(NB):
            hg_out[b] = _untile_state(sthg_s[b])
            mc_out[b] = _untile_state(stmc_s[b])
            rt_out[b] = _untile_state(strt_s[b])


def _run_layer(x, cos_tab, sin_tab, states, weights, *, NB, T, nvalid, final_norm):
    B, L, _ = x.shape
    nbb, ntc = B // NB, L // T
    R = NB * T
    s5re, s5im, hg, mc, mn, mm, rt = states

    def bspec(shape):
        nd = len(shape)
        return pl.BlockSpec((NB,) + shape[1:], lambda i, j, nd=nd: (i,) + (0,) * (nd - 1))

    def wspec(a):
        nd = a.ndim
        return pl.BlockSpec(a.shape, lambda i, j, nd=nd: (0,) * nd, pipeline_mode=pl.Buffered(1))

    in_specs = [
        pl.BlockSpec((NB, T, D_MODEL), lambda i, j: (i, j, 0)),
        pl.BlockSpec((T, 128), lambda i, j: (j, 0)),
        pl.BlockSpec((T, 128), lambda i, j: (j, 0)),
    ] + [bspec(s.shape) for s in states] + [wspec(w) for w in weights]
    out_shape = [jax.ShapeDtypeStruct(x.shape, F32)] + [jax.ShapeDtypeStruct(s.shape, F32) for s in states]
    out_specs = [pl.BlockSpec((NB, T, D_MODEL), lambda i, j: (i, j, 0))] + [bspec(s.shape) for s in states]
    scratch = [
        pltpu.VMEM((R, D_MODEL), BF16),
        pltpu.VMEM((R, PJ_COLS), F32),
        pltpu.VMEM((2 * S5_TILES, R, 128), F32),
        pltpu.VMEM((R, D_MODEL), BF16),
        pltpu.VMEM((NB, S5_N), F32),
        pltpu.VMEM((NB, S5_N), F32),
        pltpu.VMEM((NB, 256, 256), F32),
        pltpu.VMEM((NB, 256, 256), F32),
        pltpu.VMEM((NB, RET_KW, 256), F32),
        pltpu.VMEM((NB, 256), F32),
        pltpu.VMEM((NB, 256), F32),
        pltpu.VMEM((R, R), BF16),
        pltpu.VMEM((R, R), BF16),
    ]
    kern = functools.partial(_layer_kernel, NB=NB, T=T, nvalid=nvalid, final_norm=final_norm)
    outs = pl.pallas_call(
        kern,
        grid=(nbb, ntc),
        in_specs=in_specs,
        out_specs=out_specs,
        out_shape=out_shape,
        scratch_shapes=scratch,
        compiler_params=pltpu.CompilerParams(
            dimension_semantics=("arbitrary", "arbitrary"),
            vmem_limit_bytes=VMEM_LIMIT_BYTES,
        ),
    )(x, cos_tab, sin_tab, *states, *weights)
    return outs[0], outs[1:]


def _rotary_tables(pos):
    half = RET_DK // 2
    inv_freq = ROPE_BASE ** (-jnp.arange(half, dtype=F32) / half)
    ang = pos.astype(F32)[:, None] * inv_freq[None, :]
    cos, sin = jnp.cos(ang), jnp.sin(ang)
    cos_t = jnp.tile(cos, (1, 128 // half))
    sin_t = jnp.tile(jnp.concatenate([-sin, sin], axis=1), (1, 128 // RET_DK))
    return cos_t, sin_t


def _block_diag(blocks):
    g, a, b = blocks.shape
    eye = jnp.eye(g, dtype=blocks.dtype)
    return jnp.einsum('gab,gh->gahb', blocks, eye).reshape(g * a, g * b)


def _layer_weights(l, norm_w, w_in, w_out, final_norm_w, lam_re, lam_im, b_re, b_im, c_re, c_im,
                   s5_d, s5_log_dt, s5_w_glu, hgrn_lb_logits, mlstm_b_i, mlstm_b_f):
    wl = w_in[l]
    w_all = jnp.concatenate(
        [wl[:, 0:2560], wl[:, 2568:3592], wl[:, 2560:2568], jnp.zeros((D_MODEL, 120), wl.dtype)], axis=1
    ).astype(BF16)
    lam = lax.complex(lam_re[l].astype(F32), lam_im[l].astype(F32))
    dt = jnp.exp(s5_log_dt[l].astype(F32))[:, None]
    a_bar = jnp.exp(lam * dt)
    b_bar = ((a_bar - 1.0) / lam)[:, :, None] * lax.complex(b_re[l].astype(F32), b_im[l].astype(F32))
    bt = jnp.swapaxes(b_bar, 1, 2)
    bcat = jnp.concatenate([_block_diag(jnp.real(bt)), _block_diag(jnp.imag(bt))], axis=1).astype(BF16)
    ct_re = jnp.swapaxes(c_re[l].astype(F32), 1, 2)
    ct_im = jnp.swapaxes(c_im[l].astype(F32), 1, 2)
    ccat = jnp.concatenate([_block_diag(ct_re), -_block_diag(ct_im)], axis=0).astype(BF16)
    lb = jnp.cumsum(jax.nn.softmax(hgrn_lb_logits.astype(F32), axis=0), axis=0)
    lb = (lb - lb[0])[l]
    hlb = jnp.stack([jnp.log(lb), jnp.log1p(-lb), 1.0 - lb] + [jnp.zeros_like(lb)] * 5, axis=0)
    mlb = jnp.concatenate([mlstm_b_i[l].astype(F32), mlstm_b_f[l].astype(F32), jnp.zeros((120,), F32)])[None, :]
    return (
        norm_w[l].astype(F32)[None, :], w_all, w_out[l].astype(BF16), final_norm_w.astype(F32)[None, :],
        jnp.real(a_bar).reshape(1, S5_N), jnp.imag(a_bar).reshape(1, S5_N), bcat, ccat,
        s5_d[l].astype(F32)[None, :], s5_w_glu[l].astype(BF16), hlb, mlb,
    )


def _to_kernel_states(s5re, s5im, hg, mc, mn, mm, rt):
    b = s5re.shape[0]
    return (
        s5re.reshape(b, S5_N), s5im.reshape(b, S5_N),
        jnp.swapaxes(hg, 2, 3).reshape(b, 256, HEAD_W),
        mc.reshape(b, 256, HEAD_W), mn.reshape(b, 256), jnp.repeat(mm, HEAD_W, axis=1),
        rt.reshape(b, RET_KW, HEAD_W),
    )


def _from_kernel_states(states):
    s5re, s5im, hg, mc, mn, mm, rt = states
    b = s5re.shape[0]
    return (
        s5re.reshape(b, S5_GROUPS, S5_STATE), s5im.reshape(b, S5_GROUPS, S5_STATE),
        jnp.swapaxes(hg.reshape(b, N_HEADS, HEAD_W, HEAD_W), 2, 3),
        mc.reshape(b, N_HEADS, HEAD_W, HEAD_W), mn.reshape(b, N_HEADS, HEAD_W), mm[:, ::HEAD_W],
        rt.reshape(b, N_HEADS, RET_DK, HEAD_W),
    )


def _trunk(x, cos_tab, sin_tab, states, params, *, NB, T, nvalid):
    new = [[] for _ in range(7)]
    for l in range(DEPTH):
        weights = _layer_weights(l, *params)
        st_l = _to_kernel_states(*[s[l] for s in states])
        x, st_new = _run_layer(x, cos_tab, sin_tab, st_l, weights, NB=NB, T=T, nvalid=nvalid,
                               final_norm=(l == DEPTH - 1))
        for lst, val in zip(new, _from_kernel_states(st_new)):
            lst.append(val)
    return x, [jnp.stack(v, axis=0) for v in new]


def kernel(x_prompt, x_sample, state_s5_re, state_s5_im, state_hgrn, state_mlstm_C, state_mlstm_n,
           state_mlstm_m, state_ret, norm_w, w_in, w_out, final_norm_w, s5_lambda_re, s5_lambda_im,
           s5_B_re, s5_B_im, s5_C_re, s5_C_im, s5_D, s5_log_dt, s5_w_glu, hgrn_lb_logits,
           mlstm_b_i, mlstm_b_f):
    params = (norm_w, w_in, w_out, final_norm_w, s5_lambda_re, s5_lambda_im, s5_B_re, s5_B_im,
              s5_C_re, s5_C_im, s5_D, s5_log_dt, s5_w_glu, hgrn_lb_logits, mlstm_b_i, mlstm_b_f)
    bp, lp = x_prompt.shape[0], x_prompt.shape[1]
    bs, ls = x_sample.shape[0], x_sample.shape[1]

    zero_states = (
        jnp.zeros((DEPTH, bp, S5_GROUPS, S5_STATE), F32), jnp.zeros((DEPTH, bp, S5_GROUPS, S5_STATE), F32),
        jnp.zeros((DEPTH, bp, N_HEADS, HEAD_W, HEAD_W), F32), jnp.zeros((DEPTH, bp, N_HEADS, HEAD_W, HEAD_W), F32),
        jnp.zeros((DEPTH, bp, N_HEADS, HEAD_W), F32), jnp.zeros((DEPTH, bp, N_HEADS), F32),
        jnp.zeros((DEPTH, bp, N_HEADS, RET_DK, HEAD_W), F32),
    )
    cos_p, sin_p = _rotary_tables(jnp.arange(lp, dtype=jnp.int32))
    y_prompt, p_states = _trunk(x_prompt, cos_p, sin_p, zero_states, params, NB=bp, T=PROMPT_T, nvalid=PROMPT_T)

    t_pad = 8
    nb_s = CHUNK_ROWS // t_pad
    x_s = jnp.pad(x_sample, ((0, 0), (0, t_pad - ls), (0, 0)))
    cos_s, sin_s = _rotary_tables(PAST_LEN + jnp.arange(t_pad, dtype=jnp.int32))
    sample_states = (state_s5_re, state_s5_im, state_hgrn, state_mlstm_C, state_mlstm_n, state_mlstm_m, state_ret)
    y_s, d_states = _trunk(x_s, cos_s, sin_s, sample_states, params, NB=nb_s, T=t_pad, nvalid=ls)
    y_sample = y_s[:, :ls]
    return (y_prompt, y_sample, *p_states, *d_states)
```

```python
import functools
import math

import numpy as np
import jax
import jax.numpy as jnp
from jax import lax
from jax.experimental import pallas as pl
from jax.experimental.pallas import tpu as pltpu

F32 = jnp.float32
BF16 = jnp.bfloat16

D_MODEL = 1024
DEPTH = 4
BRANCH_W = 256
N_HEADS = 4
HEAD_W = 64
S5_GROUPS = 16
S5_CH = 16
S5_STATE = 64
S5_N = S5_GROUPS * S5_STATE
S5_TILES = S5_N // 128
RET_DK = 32
RET_KW = N_HEADS * RET_DK
ROPE_BASE = 10000.0
EPS = 1e-6
LOGI_PAD = -1e30
PAST_LEN = 16384

CHUNK_ROWS = 128
STACK = N_HEADS * CHUNK_ROWS
PROMPT_T = 64
SAMPLE_T = 8
VMEM_LIMIT_BYTES = 60 * 1024 * 1024

C_S5 = 0
C_HG = 512
C_ML = 1536
C_RT = 2816
C_GATE = 3584
W_ALL_COLS = 3712
HGRN_SAFE_DECAY = 80.0


def _dot(a, b):
    return jnp.dot(a, b, preferred_element_type=F32)


def _dot_nt(a, b):
    return lax.dot_general(a, b, (((1,), (1,)), ((), ())), preferred_element_type=F32)


def _dot_tn(a, b):
    return lax.dot_general(a, b, (((0,), (0,)), ((), ())), preferred_element_type=F32)


def _split3(x):
    hi = x.astype(BF16)
    r = x - hi.astype(F32)
    mid = r.astype(BF16)
    lo = (r - mid.astype(F32)).astype(BF16)
    return hi, mid, lo


def _sel_dot(sel_bf, x):
    hi, mid, lo = _split3(x)
    return _dot(sel_bf, hi) + _dot(sel_bf, mid) + _dot(sel_bf, lo)


def _dot_sel(x, sel_bf):
    hi = x.astype(BF16)
    lo = (x - hi.astype(F32)).astype(BF16)
    return _dot(hi, sel_bf) + _dot(lo, sel_bf)


def _sigmoid(x):
    return 0.5 + 0.5 * jnp.tanh(0.5 * x)


def _silu(x):
    return x * _sigmoid(x)


def _softplus_neg_abs(x):
    return jnp.log(1.0 + jnp.exp(-jnp.abs(x)))


def _log_sigmoid(x):
    return jnp.minimum(x, 0.0) - _softplus_neg_abs(x)


def _gelu_tanh(x):
    return 0.5 * x * (1.0 + jnp.tanh(math.sqrt(2.0 / math.pi) * (x + 0.044715 * (x * x * x))))


def _iota(shape, dim):
    return lax.broadcasted_iota(jnp.int32, shape, dim)


def _log2(n):
    k = int(round(math.log2(n)))
    assert (1 << k) == n, n
    return k


def _stack_heads(x_bf, head_mask):
    return jnp.where(head_mask, jnp.concatenate([x_bf] * N_HEADS, axis=0), jnp.zeros((), x_bf.dtype))


def _tile_state(s, mask):
    return jnp.where(mask, jnp.concatenate([s] * N_HEADS, axis=1), 0.0)


def _untile_state(s_bd):
    return (s_bd[:, 0:64] + s_bd[:, 64:128]) + (s_bd[:, 128:192] + s_bd[:, 192:256])


def _lanes_per_head(cols):
    return jnp.concatenate([jnp.broadcast_to(c, (CHUNK_ROWS, HEAD_W)) for c in cols], axis=1)


def _ret_log_decay(head):
    out = jnp.full(head.shape, math.log(1.0 - 2.0 ** -5.0), F32)
    for hh in range(1, N_HEADS):
        out = jnp.where(head == hh, math.log(1.0 - 2.0 ** (-5.0 - hh)), out)
    return out


N_STATES = 7
N_INPUTS = 22


def _layer_kernel(*refs, NB, T, nvalid, final_norm, n_alias):
    (x_ref, cos_ref, sin_ref, s5re_in, s5im_in, hg_in, mc_in, mn_in, mm_in, rt_in,
     nw_ref, w_ref, wo_ref, fnw_ref, are_ref, aim_ref, bcat_ref, ccat_ref, dsk_ref,
     wglu_ref, hlb_ref, mlb_ref) = refs[:N_INPUTS]
    (y_ref, s5re_out, s5im_out, hg_out, mc_out, mn_out, mm_out, rt_out,
     h_s, pj_s, bu_s, mix_s, hre_s, him_s, sthg_s, stmc_s, strt_s, stn_s, stm_s,
     perm_s, permt_s, sc_s) = refs[N_INPUTS + n_alias:]
    R = NB * T
    NCH = R // CHUNK_ROWS
    nseg = CHUNK_ROWS // T
    last = nvalid - 1
    padded = nvalid < T
    lgT = _log2(T)
    lgNB = _log2(NB)
    CR = CHUNK_ROWS
    j = pl.program_id(1)
    nj = pl.num_programs(1)

    bd_mask = (_iota((256, 256), 0) >> 6) == (_iota((256, 256), 1) >> 6)
    rt_mask = (_iota((RET_KW, 256), 0) >> 5) == (_iota((RET_KW, 256), 1) >> 6)
    hm256 = (_iota((STACK, 256), 0) >> 7) == (_iota((STACK, 256), 1) >> 6)
    hm128 = (_iota((STACK, RET_KW), 0) >> 7) == (_iota((STACK, RET_KW), 1) >> 5)
    bd_ones = bd_mask.astype(BF16)
    head_expand = hm256.astype(BF16)

    t_st = _iota((CR, STACK), 0)
    s_st = _iota((CR, STACK), 1) & (CR - 1)
    causal_st = ((t_st >> lgT) == (s_st >> lgT)) & (t_st >= s_st)
    diag_st = t_st == s_st
    t_sq = _iota((CR, CR), 0)
    s_sq = _iota((CR, CR), 1)
    causal_sq = ((t_sq >> lgT) == (s_sq >> lgT)) & (t_sq >= s_sq)
    triseg = causal_sq.astype(BF16)
    tl256 = _iota((CR, 256), 0) & (T - 1)
    tl128 = _iota((CR, 128), 0) & (T - 1)
    row256 = _iota((CR, 256), 0)
    valid256 = tl256 <= last
    valid128 = tl128 <= last

    @pl.when(j == 0)
    def _():
        ri = _iota((R, R), 0)
        ci = _iota((R, R), 1)
        perm_s[...] = (ci == (((ri & (NB - 1)) << lgT) + (ri >> lgNB))).astype(BF16)
        permt_s[...] = (ri == (((ci & (NB - 1)) << lgT) + (ci >> lgNB))).astype(BF16)
        hre_s[...] = s5re_in[...]
        him_s[...] = s5im_in[...]
        stn_s[...] = mn_in[...]
        m_in = mm_in[...]
        stm_s[...] = jnp.concatenate(
            [jnp.broadcast_to(m_in[:, hh:hh + 1], (NB, HEAD_W)) for hh in range(N_HEADS)], axis=1)
        for b in range(NB):
            sthg_s[b] = jnp.transpose(_tile_state(hg_in[b], bd_mask))
            stmc_s[b] = _tile_state(mc_in[b], bd_mask)
            strt_s[b] = _tile_state(rt_in[b], rt_mask)

    x = x_ref[...].reshape(R, D_MODEL)
    ms = jnp.mean(x * x, axis=-1, keepdims=True)
    h_s[...] = (x * lax.rsqrt(ms + EPS) * nw_ref[...]).astype(BF16)

    for i in range(C_GATE // 256):
        pj_s[:, 256 * i:256 * (i + 1)] = _dot(h_s[...], w_ref[:, 256 * i:256 * (i + 1)])
    pj_s[:, C_GATE:W_ALL_COLS] = _dot(h_s[...], w_ref[:, C_GATE:W_ALL_COLS])

    def head_norm(o):
        mu = _dot_sel(o, bd_ones) * (1.0 / HEAD_W)
        d = o - mu
        var = _dot((d * d).astype(BF16), bd_ones) * (1.0 / HEAD_W)
        return d * lax.rsqrt(var + EPS)

    def for_chunks(fn):
        for ch in range(NCH):
            fn(ch)

    def seg_last_rows(a):
        w = a.shape[1]
        pieces = [jnp.broadcast_to(a[sg * T + last:sg * T + last + 1, :], (T, w)) for sg in range(nseg)]
        return pieces[0] if nseg == 1 else jnp.concatenate(pieces, axis=0)

    def chunk_rows(ch):
        if isinstance(ch, int):
            return pl.ds(ch * CR, CR)
        return pl.ds(pl.multiple_of(ch * CR, CR), CR)

    u = pj_s[:, C_S5:C_S5 + 256]
    ub = _dot(perm_s[...], u.astype(BF16)).astype(BF16)
    for i in range(S5_TILES):
        drive = _dot(ub, bcat_ref[:, 256 * i:256 * (i + 1)])
        bu_s[2 * i] = drive[:, 0:128]
        bu_s[2 * i + 1] = drive[:, 128:256]
    a_re = are_ref[...]
    a_im = aim_ref[...]

    def s5_step(t, carry):
        idx = pl.ds(pl.multiple_of(t * NB, NB), NB)
        new = []
        for c in range(S5_TILES):
            hre, him = carry[2 * c], carry[2 * c + 1]
            ar = a_re[:, 128 * c:128 * (c + 1)]
            ai = a_im[:, 128 * c:128 * (c + 1)]
            nre = ar * hre - ai * him + bu_s[c, idx, :]
            nim = ar * him + ai * hre + bu_s[S5_TILES + c, idx, :]
            bu_s[c, idx, :] = nre
            bu_s[S5_TILES + c, idx, :] = nim
            new += [nre, nim]
        return tuple(new)

    init = []
    for c in range(S5_TILES):
        init += [hre_s[:, 128 * c:128 * (c + 1)], him_s[:, 128 * c:128 * (c + 1)]]
    fin = lax.fori_loop(0, nvalid, s5_step, tuple(init))
    for c in range(S5_TILES):
        hre_s[:, 128 * c:128 * (c + 1)] = fin[2 * c]
        him_s[:, 128 * c:128 * (c + 1)] = fin[2 * c + 1]
    hid = jnp.concatenate([bu_s[c].astype(BF16) for c in range(2 * S5_TILES)], axis=1)
    y = _sel_dot(permt_s[...], _dot(hid, ccat_ref[...])) + dsk_ref[...] * u
    y = _gelu_tanh(y)
    y = y * _sigmoid(_dot(y.astype(BF16), wglu_ref[...]))
    mix_s[:, 0:256] = (y * _silu(pj_s[:, C_S5 + 256:C_S5 + 512])).astype(BF16)

    log_lb = hlb_ref[0:1, :]
    log1m_lb = hlb_ref[1:2, :]
    one_m_lb = hlb_ref[2:3, :]

    def boundary_rows(c, m):
        blk = 2 * m
        if blk >= 8:
            c3 = c.reshape(CR // blk, blk, 256)
            return jnp.broadcast_to(c3[:, m - 1:m, :], (CR // blk, blk, 256)).reshape(CR, 256)
        c3 = c.reshape(CR // 8, 8, 256)
        sub = _iota((CR // 8, 8, 256), 1)
        out = None
        for jb in range(8 // blk):
            piece = jnp.broadcast_to(c3[:, jb * blk + m - 1:jb * blk + m, :], (CR // 8, 8, 256))
            out = piece if out is None else jnp.where((sub >> _log2(blk)) == jb, piece, out)
        return out.reshape(CR, 256)

    def hgrn_gates(rows):
        q = _silu(pj_s[rows, C_HG:C_HG + 256])
        xf = pj_s[rows, C_HG + 256:C_HG + 512]
        b_ = log1m_lb + _log_sigmoid(xf)
        log_f = jnp.maximum(log_lb, b_) + _softplus_neg_abs(log_lb - b_)
        nlf = -log_f
        hk = one_m_lb * _sigmoid(-xf)
        if padded:
            nlf = jnp.where(valid256, nlf, 0.0)
            hk = jnp.where(valid256, hk, 0.0)
        return q, hk, _sel_dot(triseg, nlf)

    def hgrn_scores_any_decay(q, hk, c):
        scores = jnp.where(diag_st, _dot_nt(q.astype(BF16), _stack_heads(hk.astype(BF16), hm256)), 0.0)
        m = T // 2
        while m >= 1:
            e = jnp.exp(-jnp.abs(c - boundary_rows(c, m)))
            second = (row256 & (2 * m - 1)) >= m
            qe = jnp.where(second, q * e, 0.0).astype(BF16)
            ke = jnp.where(second, 0.0, hk * e).astype(BF16)
            lg = _log2(2 * m)
            same_blk = (t_st >> lg) == (s_st >> lg)
            scores = scores + jnp.where(same_blk, _dot_nt(qe, _stack_heads(ke, hm256)), 0.0)
            m //= 2
        return scores

    gates = [hgrn_gates(chunk_rows(ch)) for ch in range(NCH)]
    c_max = gates[0][2]
    for ch in range(1, NCH):
        c_max = jnp.maximum(c_max, gates[ch][2])
    c_max = jnp.max(c_max)
    for ch in range(NCH):
        q, hk, c = gates[ch]
        ke = _stack_heads((hk * jnp.exp(c)).astype(BF16), hm256)
        sc_s[ch] = jnp.where(causal_st, _dot_nt((q * jnp.exp(-c)).astype(BF16), ke), 0.0)

    @pl.when(c_max > HGRN_SAFE_DECAY)
    def _():
        def redo(ch, carry):
            q, hk, c = hgrn_gates(chunk_rows(ch))
            sc_s[ch] = hgrn_scores_any_decay(q, hk, c)
            return carry
        lax.fori_loop(0, NCH, redo, 0)

    def hgrn_chunk(ch):
        rows = chunk_rows(ch)
        q, hk, c = gates[ch]
        v = pj_s[rows, C_HG + 512:C_HG + 768].astype(BF16)
        gate = pj_s[rows, C_HG + 768:C_HG + 1024]
        o = _dot(sc_s[ch].astype(BF16), _stack_heads(v, hm256))

        qd = (q * jnp.exp(-c)).astype(BF16)
        kd = hk * jnp.exp(-(seg_last_rows(c) - c))
        if padded:
            kd = jnp.where(valid256, kd, 0.0)
        kd = kd.astype(BF16)
        o_inter = []
        for sg in range(nseg):
            b = ch * nseg + sg
            sl = slice(sg * T, (sg + 1) * T)
            st = sthg_s[b]
            o_inter.append(_dot_nt(qd[sl], st.astype(BF16)))
            c_last = c[sg * T + last:sg * T + last + 1, :]
            upd = _dot_tn(v[sl], kd[sl])
            sthg_s[b] = st * jnp.exp(-c_last) + jnp.where(bd_mask, upd, 0.0)
        o = o + (o_inter[0] if nseg == 1 else jnp.concatenate(o_inter, axis=0))
        mix_s[rows, 256:512] = (head_norm(o) * _silu(gate)).astype(BF16)

    for_chunks(hgrn_chunk)

    lane128 = _iota((CR, 128), 1)
    is_i = lane128 < N_HEADS
    is_f = (lane128 >= N_HEADS) & (lane128 < 2 * N_HEADS)

    def mlstm_chunk(ch):
        rows = chunk_rows(ch)
        q = pj_s[rows, C_ML:C_ML + 256]
        k = pj_s[rows, C_ML + 256:C_ML + 512] * (HEAD_W ** -0.5)
        v = pj_s[rows, C_ML + 512:C_ML + 768].astype(BF16)
        og = pj_s[rows, C_ML + 768:C_ML + 1024]
        gate = pj_s[rows, C_ML + 1024:C_ML + 1280]
        gpre = pj_s[rows, C_GATE:W_ALL_COLS] + mlb_ref[...]
        log_i = gpre
        log_fg = _log_sigmoid(gpre)
        if padded:
            log_i = jnp.where(valid128, log_i, LOGI_PAD)
            log_fg = jnp.where(valid128, log_fg, 0.0)
        cum = _sel_dot(triseg, jnp.where(is_f, log_fg, 0.0))
        g = jnp.where(is_i, log_i, cum)
        g_t = jnp.transpose(g)

        if nseg == 1:
            ms_rows = jnp.broadcast_to(stm_s[pl.ds(ch, 1), :], (CR, 256))
        else:
            ms_rows = jnp.concatenate(
                [jnp.broadcast_to(stm_s[pl.ds(ch * nseg + sg, 1), :], (T, 256)) for sg in range(nseg)], axis=0)

        d_blocks, mt_cols, wp_cols, cf_cols, ig_cols = [], [], [], [], []
        for hh in range(N_HEADS):
            cf_col = g[:, N_HEADS + hh:N_HEADS + hh + 1]
            ig_col = g[:, hh:hh + 1]
            row_f = jnp.broadcast_to(g_t[N_HEADS + hh:N_HEADS + hh + 1, :], (CR, CR))
            row_i = jnp.broadcast_to(g_t[hh:hh + 1, :], (CR, CR))
            log_d = jnp.where(causal_sq, jnp.broadcast_to(cf_col, (CR, CR)) - row_f + row_i, -jnp.inf)
            w_prev = cf_col + ms_rows[:, hh * HEAD_W:hh * HEAD_W + 1]
            m_t = jnp.maximum(w_prev, jnp.max(log_d, axis=1, keepdims=True))
            d_blocks.append(jnp.exp(log_d - m_t))
            mt_cols.append(m_t)
            wp_cols.append(w_prev)
            cf_cols.append(cf_col)
            ig_cols.append(ig_col)
        d_mat = jnp.concatenate(d_blocks, axis=1)
        mt = _lanes_per_head(mt_cols)
        s_prev = jnp.exp(_lanes_per_head(wp_cols) - mt)
        cumf = _lanes_per_head(cf_cols)
        lig = _lanes_per_head(ig_cols)

        qb = q.astype(BF16)
        scores = _dot_nt(qb, _stack_heads(k.astype(BF16), hm256)) * d_mat
        num = _dot(scores.astype(BF16), _stack_heads(v, hm256))
        den = _dot_sel(scores, head_expand)

        w_k = jnp.exp(seg_last_rows(cumf) - cumf + lig - seg_last_rows(mt))
        if padded:
            w_k = jnp.where(valid256, w_k, 0.0)
        kw = w_k * k
        kwb = kw.astype(BF16)
        qc, qn = [], []
        for sg in range(nseg):
            b = ch * nseg + sg
            sl = slice(sg * T, (sg + 1) * T)
            cs = stmc_s[b]
            ns = stn_s[pl.ds(b, 1), :]
            ms_b = stm_s[pl.ds(b, 1), :]
            qc.append(_dot(qb[sl], cs.astype(BF16)))
            qn.append(_dot_sel(q[sl] * ns, bd_ones))
            lr = sg * T + last
            m_new = mt[lr:lr + 1, :]
            carry = jnp.exp(cumf[lr:lr + 1, :] + ms_b - m_new)
            stmc_s[b] = carry * cs + jnp.where(bd_mask, _dot_tn(kwb[sl], v[sl]), 0.0)
            stn_s[pl.ds(b, 1), :] = carry * ns + jnp.sum(kw[sl], axis=0, keepdims=True)
            stm_s[pl.ds(b, 1), :] = m_new
        qc = qc[0] if nseg == 1 else jnp.concatenate(qc, axis=0)
        qn = qn[0] if nseg == 1 else jnp.concatenate(qn, axis=0)
        num = s_prev * qc + num
        den = s_prev * qn + den
        hcell = num / jnp.maximum(jnp.abs(den), jnp.exp(-mt))
        mix_s[rows, 512:768] = (head_norm(_sigmoid(og) * hcell) * _silu(gate)).astype(BF16)

    for_chunks(mlstm_chunk)

    gam_st = _ret_log_decay(_iota((CR, STACK), 1) >> 7)
    d_ret = jnp.where(causal_st, jnp.exp(((t_st - s_st) & (T - 1)).astype(F32) * gam_st), 0.0)
    gam128 = _ret_log_decay(_iota((CR, 128), 1) >> 5)
    gam256 = _ret_log_decay(_iota((1, 256), 1) >> 6)
    q_dec = jnp.exp((tl128 + 1).astype(F32) * gam128)
    k_dec = jnp.where(valid128, jnp.exp((last - tl128).astype(F32) * gam128), 0.0)
    s_dec = jnp.exp(float(last + 1) * gam256)
    first_half = (_iota((CR, 128), 1) & (RET_DK - 1)) < (RET_DK // 2)

    def rotary(t, cos, sin_signed):
        partner = jnp.where(first_half, pltpu.roll(t, 128 - RET_DK // 2, axis=1), pltpu.roll(t, RET_DK // 2, axis=1))
        return t * cos + partner * sin_signed

    def ret_chunk(ch):
        rows = chunk_rows(ch)
        cos = jnp.concatenate([cos_ref[...]] * nseg, axis=0)
        sin_signed = jnp.concatenate([sin_ref[...]] * nseg, axis=0)
        q = rotary(pj_s[rows, C_RT:C_RT + 128], cos, sin_signed)
        k = rotary(pj_s[rows, C_RT + 128:C_RT + 256], cos, sin_signed) * (RET_DK ** -0.5)
        v = pj_s[rows, C_RT + 256:C_RT + 512].astype(BF16)
        gate = pj_s[rows, C_RT + 512:C_RT + 768]
        scores = _dot_nt(q.astype(BF16), _stack_heads(k.astype(BF16), hm128)) * d_ret
        o = _dot(scores.astype(BF16), _stack_heads(v, hm256))
        qd = (q * q_dec).astype(BF16)
        kd = (k * k_dec).astype(BF16)
        o_inter = []
        for sg in range(nseg):
            b = ch * nseg + sg
            sl = slice(sg * T, (sg + 1) * T)
            st = strt_s[b]
            o_inter.append(_dot(qd[sl], st.astype(BF16)))
            strt_s[b] = st * s_dec + jnp.where(rt_mask, _dot_tn(kd[sl], v[sl]), 0.0)
        o = o + (o_inter[0] if nseg == 1 else jnp.concatenate(o_inter, axis=0))
        mix_s[rows, 768:1024] = (head_norm(o) * _silu(gate)).astype(BF16)

    for_chunks(ret_chunk)

    out = x_ref[...].reshape(R, D_MODEL) + _dot(mix_s[...], wo_ref[...])
    if final_norm:
        ms2 = jnp.mean(out * out, axis=-1, keepdims=True)
        out = out * lax.rsqrt(ms2 + EPS) * fnw_ref[...]
    y_ref[...] = out.reshape(NB, T, D_MODEL)

    @pl.when(j == nj - 1)
    def _():
        s5re_out[...] = hre_s[...]
        s5im_out[...] = him_s[...]
        mn_out[...] = stn_s[...]
        mm_out[...] = jnp.concatenate([stm_s[:, hh * HEAD_W:hh * HEAD_W + 1] for hh in range(N_HEADS)], axis=1)
        for b in range(NB):
            hg_out[b] = _untile_state(jnp.transpose(sthg_s[b]))
            mc_out[b] = _untile_state(stmc_s[b])
            rt_out[b] = _untile_state(strt_s[b])


def _run_layer(l, x, cos_tab, sin_tab, states, prev_out, weights, *, NB, T, nvalid):
    B, L, _ = x.shape
    nbb, ntc = B // NB, L // T
    R = NB * T

    def layer_spec(shape, first, single_buffer=False):
        rest = tuple(shape[2:])
        zeros = (0,) * len(rest)
        kw = {"pipeline_mode": pl.Buffered(1)} if single_buffer else {}
        if first is None:
            return pl.BlockSpec((None, shape[1]) + rest, lambda i, j: (l, 0) + zeros, **kw)
        return pl.BlockSpec((None, first) + rest, lambda i, j: (l, i) + zeros, **kw)

    state_specs = [layer_spec(s.shape, NB) for s in states]
    in_specs = [
        pl.BlockSpec((NB, T, D_MODEL), lambda i, j: (i, j, 0)),
        pl.BlockSpec((T, 128), lambda i, j: (j, 0)),
        pl.BlockSpec((T, 128), lambda i, j: (j, 0)),
    ] + state_specs + [layer_spec(w.shape, None, single_buffer=True) for w in weights]
    assert len(in_specs) == N_INPUTS
    operands = [x, cos_tab, sin_tab, *states, *weights]
    aliases = {}
    if prev_out is not None:
        in_specs += [pl.BlockSpec(memory_space=pl.ANY)] * N_STATES
        aliases = {N_INPUTS + k: 1 + k for k in range(N_STATES)}
        operands += list(prev_out)
    out_shape = [jax.ShapeDtypeStruct(x.shape, F32)] + [jax.ShapeDtypeStruct(s.shape, F32) for s in states]
    out_specs = [pl.BlockSpec((NB, T, D_MODEL), lambda i, j: (i, j, 0))] + state_specs
    scratch = [
        pltpu.VMEM((R, D_MODEL), BF16),
        pltpu.VMEM((R, W_ALL_COLS), F32),
        pltpu.VMEM((2 * S5_TILES, R, 128), F32),
        pltpu.VMEM((R, D_MODEL), BF16),
        pltpu.VMEM((NB, S5_N), F32),
        pltpu.VMEM((NB, S5_N), F32),
        pltpu.VMEM((NB, 256, 256), F32),
        pltpu.VMEM((NB, 256, 256), F32),
        pltpu.VMEM((NB, RET_KW, 256), F32),
        pltpu.VMEM((NB, 256), F32),
        pltpu.VMEM((NB, 256), F32),
        pltpu.VMEM((R, R), BF16),
        pltpu.VMEM((R, R), BF16),
        pltpu.VMEM((R // CHUNK_ROWS, CHUNK_ROWS, STACK), F32),
    ]
    kern = functools.partial(_layer_kernel, NB=NB, T=T, nvalid=nvalid, final_norm=(l == DEPTH - 1),
                             n_alias=len(aliases))
    outs = pl.pallas_call(
        kern,
        grid=(nbb, ntc),
        in_specs=in_specs,
        out_specs=out_specs,
        out_shape=out_shape,
        scratch_shapes=scratch,
        input_output_aliases=aliases,
        compiler_params=pltpu.CompilerParams(
            dimension_semantics=("arbitrary", "arbitrary"),
            vmem_limit_bytes=VMEM_LIMIT_BYTES,
        ),
    )(*operands)
    return outs[0], outs[1:]


def _rotary_tables(pos):
    half = RET_DK // 2
    inv_freq = ROPE_BASE ** (-jnp.arange(half, dtype=F32) / half)
    ang = pos.astype(F32)[:, None] * inv_freq[None, :]
    cos, sin = jnp.cos(ang), jnp.sin(ang)
    cos_t = jnp.tile(cos, (1, 128 // half))
    sin_t = jnp.tile(jnp.concatenate([-sin, sin], axis=1), (1, 128 // RET_DK))
    return cos_t, sin_t


def _block_diag(blocks):
    d, g, a, b = blocks.shape
    eye = jnp.eye(g, dtype=blocks.dtype)
    return jnp.einsum('dgab,gh->dgahb', blocks, eye).reshape(d, g * a, g * b)


def _prepare_weights(norm_w, w_in, w_out, final_norm_w, lam_re, lam_im, b_re, b_im, c_re, c_im,
                     s5_d, s5_log_dt, s5_w_glu, hgrn_lb_logits, mlstm_b_i, mlstm_b_f):
    depth = w_in.shape[0]
    w_all = jnp.concatenate(
        [w_in[:, :, 0:2560], w_in[:, :, 2568:3592], w_in[:, :, 2560:2568],
         jnp.zeros((depth, D_MODEL, W_ALL_COLS - 3592), w_in.dtype)], axis=2).astype(BF16)
    lam = lax.complex(lam_re.astype(F32), lam_im.astype(F32))
    dt = jnp.exp(s5_log_dt.astype(F32))[:, :, None]
    a_bar = jnp.exp(lam * dt)
    b_bar = ((a_bar - 1.0) / lam)[..., None] * lax.complex(b_re.astype(F32), b_im.astype(F32))
    bt = jnp.swapaxes(b_bar, 2, 3)
    bcat = jnp.concatenate([_block_diag(jnp.real(bt)), _block_diag(jnp.imag(bt))], axis=2).astype(BF16)
    ct_re = jnp.swapaxes(c_re.astype(F32), 2, 3)
    ct_im = jnp.swapaxes(c_im.astype(F32), 2, 3)
    ccat = jnp.concatenate([_block_diag(ct_re), -_block_diag(ct_im)], axis=1).astype(BF16)
    lb = jnp.cumsum(jax.nn.softmax(hgrn_lb_logits.astype(F32), axis=0), axis=0)
    lb = lb - lb[0]
    hlb = jnp.stack([jnp.log(lb), jnp.log1p(-lb), 1.0 - lb] + [jnp.zeros_like(lb)] * 5, axis=1)
    mlb = jnp.concatenate([mlstm_b_i.astype(F32), mlstm_b_f.astype(F32), jnp.zeros((depth, 120), F32)], axis=1)
    return (
        norm_w.astype(F32)[:, None, :], w_all, w_out.astype(BF16),
        jnp.broadcast_to(final_norm_w.astype(F32)[None, None, :], (depth, 1, D_MODEL)),
        jnp.real(a_bar).reshape(depth, 1, S5_N), jnp.imag(a_bar).reshape(depth, 1, S5_N), bcat, ccat,
        s5_d.astype(F32)[:, None, :], s5_w_glu.astype(BF16), hlb, mlb[:, None, :],
    )


def _trunk(x, pos, states, weights, *, NB, T, nvalid):
    s5re, s5im, hg, mc, mn, mm, rt = states
    d, b = s5re.shape[0], s5re.shape[1]
    kstates = (s5re.reshape(d, b, S5_N), s5im.reshape(d, b, S5_N), hg.reshape(d, b, 256, HEAD_W),
               mc.reshape(d, b, 256, HEAD_W), mn.reshape(d, b, 256), mm, rt.reshape(d, b, RET_KW, HEAD_W))
    cos_tab, sin_tab = _rotary_tables(pos)
    out = None
    for l in range(DEPTH):
        x, out = _run_layer(l, x, cos_tab, sin_tab, kstates, out, weights, NB=NB, T=T, nvalid=nvalid)
    o_s5re, o_s5im, o_hg, o_mc, o_mn, o_mm, o_rt = out
    return x, [
        o_s5re.reshape(d, b, S5_GROUPS, S5_STATE), o_s5im.reshape(d, b, S5_GROUPS, S5_STATE),
        o_hg.reshape(d, b, N_HEADS, HEAD_W, HEAD_W), o_mc.reshape(d, b, N_HEADS, HEAD_W, HEAD_W),
        o_mn.reshape(d, b, N_HEADS, HEAD_W), o_mm, o_rt.reshape(d, b, N_HEADS, RET_DK, HEAD_W),
    ]


def _prompt_trunk(x, states, weights):
    b, l, _ = x.shape
    return _trunk(x, jnp.arange(l, dtype=jnp.int32), states, weights, NB=b, T=PROMPT_T, nvalid=PROMPT_T)


def _sample_trunk(x, states, weights):
    ls = x.shape[1]
    x_pad = jnp.pad(x, ((0, 0), (0, SAMPLE_T - ls), (0, 0)))
    y, new = _trunk(x_pad, PAST_LEN + jnp.arange(SAMPLE_T, dtype=jnp.int32), states, weights,
                    NB=CHUNK_ROWS // SAMPLE_T, T=SAMPLE_T, nvalid=ls)
    return y[:, :ls], new


def kernel(x_prompt, x_sample, state_s5_re, state_s5_im, state_hgrn, state_mlstm_C, state_mlstm_n,
           state_mlstm_m, state_ret, norm_w, w_in, w_out, final_norm_w, s5_lambda_re, s5_lambda_im,
           s5_B_re, s5_B_im, s5_C_re, s5_C_im, s5_D, s5_log_dt, s5_w_glu, hgrn_lb_logits,
           mlstm_b_i, mlstm_b_f):
    weights = _prepare_weights(norm_w, w_in, w_out, final_norm_w, s5_lambda_re, s5_lambda_im, s5_B_re, s5_B_im,
                               s5_C_re, s5_C_im, s5_D, s5_log_dt, s5_w_glu, hgrn_lb_logits, mlstm_b_i, mlstm_b_f)
    bp = x_prompt.shape[0]
    zero_states = (
        jnp.zeros((DEPTH, bp, S5_GROUPS, S5_STATE), F32), jnp.zeros((DEPTH, bp, S5_GROUPS, S5_STATE), F32),
        jnp.zeros((DEPTH, bp, N_HEADS, HEAD_W, HEAD_W), F32), jnp.zeros((DEPTH, bp, N_HEADS, HEAD_W, HEAD_W), F32),
        jnp.zeros((DEPTH, bp, N_HEADS, HEAD_W), F32), jnp.zeros((DEPTH, bp, N_HEADS), F32),
        jnp.zeros((DEPTH, bp, N_HEADS, RET_DK, HEAD_W), F32),
    )
    y_prompt, p_states = _prompt_trunk(x_prompt, zero_states, weights)
    sample_states = (state_s5_re, state_s5_im, state_hgrn, state_mlstm_C, state_mlstm_n, state_mlstm_m, state_ret)
    y_sample, d_states = _sample_trunk(x_sample, sample_states, weights)
    return (y_prompt, y_sample, *p_states, *d_states)
```

```python
import functools
import math

import numpy as np
import jax
import jax.numpy as jnp
from jax import lax
from jax.experimental import pallas as pl
from jax.experimental.pallas import tpu as pltpu

F32 = jnp.float32
BF16 = jnp.bfloat16

D_MODEL = 1024
DEPTH = 4
BRANCH_W = 256
N_HEADS = 4
HEAD_W = 64
S5_GROUPS = 16
S5_CH = 16
S5_STATE = 64
S5_N = S5_GROUPS * S5_STATE
S5_TILES = S5_N // 128
RET_DK = 32
RET_KW = N_HEADS * RET_DK
ROPE_BASE = 10000.0
EPS = 1e-6
LOGI_PAD = -1e30
PAST_LEN = 16384

CHUNK_ROWS = 128
STACK = N_HEADS * CHUNK_ROWS
PROMPT_T = 64
SAMPLE_T = 8
VMEM_LIMIT_BYTES = 60 * 1024 * 1024

C_S5 = 0
C_HG = 512
C_ML = 1536
C_RT = 2816
C_GATE = 3584
W_ALL_COLS = 3712
HGRN_SAFE_DECAY = 80.0


def _dot(a, b):
    return jnp.dot(a, b, preferred_element_type=F32)


def _dot_nt(a, b):
    return lax.dot_general(a, b, (((1,), (1,)), ((), ())), preferred_element_type=F32)


def _dot_tn(a, b):
    return lax.dot_general(a, b, (((0,), (0,)), ((), ())), preferred_element_type=F32)


def _split3(x):
    hi = x.astype(BF16)
    r = x - hi.astype(F32)
    mid = r.astype(BF16)
    lo = (r - mid.astype(F32)).astype(BF16)
    return hi, mid, lo


def _sel_dot(sel_bf, x):
    hi, mid, lo = _split3(x)
    return _dot(sel_bf, hi) + _dot(sel_bf, mid) + _dot(sel_bf, lo)


def _dot_sel(x, sel_bf):
    hi = x.astype(BF16)
    lo = (x - hi.astype(F32)).astype(BF16)
    return _dot(hi, sel_bf) + _dot(lo, sel_bf)


def _sigmoid(x):
    return 0.5 + 0.5 * jnp.tanh(0.5 * x)


def _silu(x):
    return x * _sigmoid(x)


def _softplus_neg_abs(x):
    return jnp.log(1.0 + jnp.exp(-jnp.abs(x)))


def _log_sigmoid(x):
    return jnp.minimum(x, 0.0) - _softplus_neg_abs(x)


def _gelu_tanh(x):
    return 0.5 * x * (1.0 + jnp.tanh(math.sqrt(2.0 / math.pi) * (x + 0.044715 * (x * x * x))))


def _iota(shape, dim):
    return lax.broadcasted_iota(jnp.int32, shape, dim)


def _log2(n):
    k = int(round(math.log2(n)))
    assert (1 << k) == n, n
    return k


def _stack_heads(x_bf, head_mask):
    return jnp.where(head_mask, jnp.concatenate([x_bf] * N_HEADS, axis=0), jnp.zeros((), x_bf.dtype))


def _tile_state(s, mask):
    return jnp.where(mask, jnp.concatenate([s] * N_HEADS, axis=1), 0.0)


def _untile_state(s_bd):
    return (s_bd[:, 0:64] + s_bd[:, 64:128]) + (s_bd[:, 128:192] + s_bd[:, 192:256])


def _lanes_per_head(cols):
    return jnp.concatenate([jnp.broadcast_to(c, (CHUNK_ROWS, HEAD_W)) for c in cols], axis=1)


def _ret_log_decay(head):
    out = jnp.full(head.shape, math.log(1.0 - 2.0 ** -5.0), F32)
    for hh in range(1, N_HEADS):
        out = jnp.where(head == hh, math.log(1.0 - 2.0 ** (-5.0 - hh)), out)
    return out


N_STATES = 7
N_INPUTS = 23


def _layer_kernel(*refs, NB, T, nvalid, final_norm, n_alias, skew):
    (x_ref, xn_ref, cos_ref, sin_ref, s5re_in, s5im_in, hg_in, mc_in, mn_in, mm_in, rt_in,
     nw_ref, w_ref, wo_ref, fnw_ref, are_ref, aim_ref, bcat_ref, ccat_ref, dsk_ref,
     wglu_ref, hlb_ref, mlb_ref) = refs[:N_INPUTS]
    (y_ref, s5re_out, s5im_out, hg_out, mc_out, mn_out, mm_out, rt_out,
     pj_s, pjn_s, hn_s, bu_s, mix_s, hre_s, him_s, sthg_s, stmc_s, strt_s, stn_s, stm_s,
     perm_s, permt_s, sc_s) = refs[N_INPUTS + n_alias:]
    R = NB * T
    NCH = R // CHUNK_ROWS
    nseg = CHUNK_ROWS // T
    last = nvalid - 1
    padded = nvalid < T
    lgT = _log2(T)
    lgNB = _log2(NB)
    CR = CHUNK_ROWS
    j = pl.program_id(1)
    nj = pl.num_programs(1)

    bd_mask = (_iota((256, 256), 0) >> 6) == (_iota((256, 256), 1) >> 6)
    rt_mask = (_iota((RET_KW, 256), 0) >> 5) == (_iota((RET_KW, 256), 1) >> 6)
    hm256 = (_iota((STACK, 256), 0) >> 7) == (_iota((STACK, 256), 1) >> 6)
    hm128 = (_iota((STACK, RET_KW), 0) >> 7) == (_iota((STACK, RET_KW), 1) >> 5)
    bd_ones = bd_mask.astype(BF16)
    head_expand = hm256.astype(BF16)

    t_st = _iota((CR, STACK), 0)
    s_st = _iota((CR, STACK), 1) & (CR - 1)
    causal_st = ((t_st >> lgT) == (s_st >> lgT)) & (t_st >= s_st)
    diag_st = t_st == s_st
    t_sq = _iota((CR, CR), 0)
    s_sq = _iota((CR, CR), 1)
    causal_sq = ((t_sq >> lgT) == (s_sq >> lgT)) & (t_sq >= s_sq)
    triseg = causal_sq.astype(BF16)
    tl256 = _iota((CR, 256), 0) & (T - 1)
    tl128 = _iota((CR, 128), 0) & (T - 1)
    row256 = _iota((CR, 256), 0)
    valid256 = tl256 <= last
    valid128 = tl128 <= last

    @pl.when(j == 0)
    def _():
        ri = _iota((R, R), 0)
        ci = _iota((R, R), 1)
        perm_s[...] = (ci == (((ri & (NB - 1)) << lgT) + (ri >> lgNB))).astype(BF16)
        permt_s[...] = (ri == (((ci & (NB - 1)) << lgT) + (ci >> lgNB))).astype(BF16)
        hre_s[...] = s5re_in[...]
        him_s[...] = s5im_in[...]
        stn_s[...] = mn_in[...]
        m_in = mm_in[...]
        stm_s[...] = jnp.concatenate(
            [jnp.broadcast_to(m_in[:, hh:hh + 1], (NB, HEAD_W)) for hh in range(N_HEADS)], axis=1)
        for b in range(NB):
            sthg_s[b] = jnp.transpose(_tile_state(hg_in[b].reshape(256, HEAD_W), bd_mask))
            stmc_s[b] = _tile_state(mc_in[b].reshape(256, HEAD_W), bd_mask)
            strt_s[b] = _tile_state(rt_in[b].reshape(RET_KW, HEAD_W), rt_mask)

    def project(src_ref, dst_ref):
        xs = src_ref[...].reshape(R, D_MODEL)
        ms = jnp.mean(xs * xs, axis=-1, keepdims=True)
        hn = (xs * lax.rsqrt(ms + EPS) * nw_ref[...]).astype(BF16)
        for i in range(C_GATE // 256):
            dst_ref[:, 256 * i:256 * (i + 1)] = _dot(hn, w_ref[:, 256 * i:256 * (i + 1)])
        dst_ref[:, C_GATE:W_ALL_COLS] = _dot(hn, w_ref[:, C_GATE:W_ALL_COLS])

    @pl.when(j == 0)
    def _():
        project(x_ref, pj_s)

    fillers = []
    if skew:
        xs = xn_ref[...].reshape(R, D_MODEL)
        ms_n = jnp.mean(xs * xs, axis=-1, keepdims=True)
        hn_s[...] = (xs * lax.rsqrt(ms_n + EPS) * nw_ref[...]).astype(BF16)

        def proj_tile(c0, c1):
            def task():
                pjn_s[:, c0:c1] = _dot(hn_s[...], w_ref[:, c0:c1])
            return task

        fillers = [proj_tile(256 * i, 256 * (i + 1)) for i in range(C_GATE // 256)]
        fillers.append(proj_tile(C_GATE, W_ALL_COLS))

    def fill(n=1):
        for _ in range(min(n, len(fillers))):
            fillers.pop(0)()

    def head_norm(o):
        mu = _dot_sel(o, bd_ones) * (1.0 / HEAD_W)
        d = o - mu
        var = _dot((d * d).astype(BF16), bd_ones) * (1.0 / HEAD_W)
        return d * lax.rsqrt(var + EPS)

    def for_chunks(fn):
        for ch in range(NCH):
            fill()
            fn(ch)

    def seg_last_rows(a):
        w = a.shape[1]
        pieces = [jnp.broadcast_to(a[sg * T + last:sg * T + last + 1, :], (T, w)) for sg in range(nseg)]
        return pieces[0] if nseg == 1 else jnp.concatenate(pieces, axis=0)

    def chunk_rows(ch):
        if isinstance(ch, int):
            return pl.ds(ch * CR, CR)
        return pl.ds(pl.multiple_of(ch * CR, CR), CR)

    u = pj_s[:, C_S5:C_S5 + 256]
    ub = _dot(perm_s[...], u.astype(BF16)).astype(BF16)
    for i in range(S5_TILES):
        drive = _dot(ub, bcat_ref[:, 256 * i:256 * (i + 1)])
        bu_s[2 * i] = drive[:, 0:128]
        bu_s[2 * i + 1] = drive[:, 128:256]
    a_re = are_ref[...]
    a_im = aim_ref[...]

    def s5_step(t, carry):
        idx = pl.ds(t * NB, NB)
        new = []
        for c in range(S5_TILES):
            hre, him = carry[2 * c], carry[2 * c + 1]
            ar = a_re[:, 128 * c:128 * (c + 1)]
            ai = a_im[:, 128 * c:128 * (c + 1)]
            nre = ar * hre - ai * him + bu_s[c, idx, :]
            nim = ar * him + ai * hre + bu_s[S5_TILES + c, idx, :]
            bu_s[c, idx, :] = nre
            bu_s[S5_TILES + c, idx, :] = nim
            new += [nre, nim]
        return tuple(new)

    init = []
    for c in range(S5_TILES):
        init += [hre_s[:, 128 * c:128 * (c + 1)], him_s[:, 128 * c:128 * (c + 1)]]
    fill()
    fin = tuple(init)
    for t in range(nvalid):
        fin = s5_step(t, fin)
    for c in range(S5_TILES):
        hre_s[:, 128 * c:128 * (c + 1)] = fin[2 * c]
        him_s[:, 128 * c:128 * (c + 1)] = fin[2 * c + 1]
    hid = jnp.concatenate([bu_s[c].astype(BF16) for c in range(2 * S5_TILES)], axis=1)
    y = _sel_dot(permt_s[...], _dot(hid, ccat_ref[...])) + dsk_ref[...] * u
    fill()
    y = _gelu_tanh(y)
    y = y * _sigmoid(_dot(y.astype(BF16), wglu_ref[...]))
    mix_s[:, 0:256] = (y * _silu(pj_s[:, C_S5 + 256:C_S5 + 512])).astype(BF16)

    log_lb = hlb_ref[0:1, :]
    log1m_lb = hlb_ref[1:2, :]
    one_m_lb = hlb_ref[2:3, :]

    def boundary_rows(c, m):
        blk = 2 * m
        if blk >= 8:
            c3 = c.reshape(CR // blk, blk, 256)
            return jnp.broadcast_to(c3[:, m - 1:m, :], (CR // blk, blk, 256)).reshape(CR, 256)
        c3 = c.reshape(CR // 8, 8, 256)
        sub = _iota((CR // 8, 8, 256), 1)
        out = None
        for jb in range(8 // blk):
            piece = jnp.broadcast_to(c3[:, jb * blk + m - 1:jb * blk + m, :], (CR // 8, 8, 256))
            out = piece if out is None else jnp.where((sub >> _log2(blk)) == jb, piece, out)
        return out.reshape(CR, 256)

    def hgrn_gates(rows):
        q = _silu(pj_s[rows, C_HG:C_HG + 256])
        xf = pj_s[rows, C_HG + 256:C_HG + 512]
        b_ = log1m_lb + _log_sigmoid(xf)
        log_f = jnp.maximum(log_lb, b_) + _softplus_neg_abs(log_lb - b_)
        nlf = -log_f
        hk = one_m_lb * _sigmoid(-xf)
        if padded:
            nlf = jnp.where(valid256, nlf, 0.0)
            hk = jnp.where(valid256, hk, 0.0)
        return q, hk, _sel_dot(triseg, nlf)

    def hgrn_scores_any_decay(q, hk, c):
        scores = jnp.where(diag_st, _dot_nt(q.astype(BF16), _stack_heads(hk.astype(BF16), hm256)), 0.0)
        m = T // 2
        while m >= 1:
            e = jnp.exp(-jnp.abs(c - boundary_rows(c, m)))
            second = (row256 & (2 * m - 1)) >= m
            qe = jnp.where(second, q * e, 0.0).astype(BF16)
            ke = jnp.where(second, 0.0, hk * e).astype(BF16)
            lg = _log2(2 * m)
            same_blk = (t_st >> lg) == (s_st >> lg)
            scores = scores + jnp.where(same_blk, _dot_nt(qe, _stack_heads(ke, hm256)), 0.0)
            m //= 2
        return scores

    fill()
    gates = [hgrn_gates(chunk_rows(ch)) for ch in range(NCH)]
    c_max = gates[0][2]
    for ch in range(1, NCH):
        c_max = jnp.maximum(c_max, gates[ch][2])
    c_max = jnp.max(c_max)
    for ch in range(NCH):
        q, hk, c = gates[ch]
        ke = _stack_heads((hk * jnp.exp(c)).astype(BF16), hm256)
        sc_s[ch] = jnp.where(causal_st, _dot_nt((q * jnp.exp(-c)).astype(BF16), ke), 0.0)

    @pl.when(c_max > HGRN_SAFE_DECAY)
    def _():
        def redo(ch, carry):
            q, hk, c = hgrn_gates(chunk_rows(ch))
            sc_s[ch] = hgrn_scores_any_decay(q, hk, c)
            return carry
        lax.fori_loop(0, NCH, redo, 0)

    def hgrn_chunk(ch):
        rows = chunk_rows(ch)
        q, hk, c = gates[ch]
        v = pj_s[rows, C_HG + 512:C_HG + 768].astype(BF16)
        gate = pj_s[rows, C_HG + 768:C_HG + 1024]
        o = _dot(sc_s[ch].astype(BF16), _stack_heads(v, hm256))

        qd = (q * jnp.exp(-c)).astype(BF16)
        kd = hk * jnp.exp(-(seg_last_rows(c) - c))
        if padded:
            kd = jnp.where(valid256, kd, 0.0)
        kd = kd.astype(BF16)
        o_inter = []
        for sg in range(nseg):
            b = ch * nseg + sg
            sl = slice(sg * T, (sg + 1) * T)
            st = sthg_s[b]
            o_inter.append(_dot_nt(qd[sl], st.astype(BF16)))
            c_last = c[sg * T + last:sg * T + last + 1, :]
            upd = _dot_tn(v[sl], kd[sl])
            sthg_s[b] = st * jnp.exp(-c_last) + jnp.where(bd_mask, upd, 0.0)
        o = o + (o_inter[0] if nseg == 1 else jnp.concatenate(o_inter, axis=0))
        mix_s[rows, 256:512] = (head_norm(o) * _silu(gate)).astype(BF16)

    for_chunks(hgrn_chunk)

    lane128 = _iota((CR, 128), 1)
    is_i = lane128 < N_HEADS
    is_f = (lane128 >= N_HEADS) & (lane128 < 2 * N_HEADS)

    def mlstm_chunk(ch):
        rows = chunk_rows(ch)
        q = pj_s[rows, C_ML:C_ML + 256]
        k = pj_s[rows, C_ML + 256:C_ML + 512] * (HEAD_W ** -0.5)
        v = pj_s[rows, C_ML + 512:C_ML + 768].astype(BF16)
        og = pj_s[rows, C_ML + 768:C_ML + 1024]
        gate = pj_s[rows, C_ML + 1024:C_ML + 1280]
        gpre = pj_s[rows, C_GATE:W_ALL_COLS] + mlb_ref[...]
        log_i = gpre
        log_fg = _log_sigmoid(gpre)
        if padded:
            log_i = jnp.where(valid128, log_i, LOGI_PAD)
            log_fg = jnp.where(valid128, log_fg, 0.0)
        cum = _sel_dot(triseg, jnp.where(is_f, log_fg, 0.0))
        g = jnp.where(is_i, log_i, cum)
        g_t = jnp.transpose(g)

        if nseg == 1:
            ms_rows = jnp.broadcast_to(stm_s[pl.ds(ch, 1), :], (CR, 256))
        else:
            ms_rows = jnp.concatenate(
                [jnp.broadcast_to(stm_s[pl.ds(ch * nseg + sg, 1), :], (T, 256)) for sg in range(nseg)], axis=0)

        d_blocks, mt_cols, wp_cols, cf_cols, ig_cols = [], [], [], [], []
        for hh in range(N_HEADS):
            cf_col = g[:, N_HEADS + hh:N_HEADS + hh + 1]
            ig_col = g[:, hh:hh + 1]
            row_f = jnp.broadcast_to(g_t[N_HEADS + hh:N_HEADS + hh + 1, :], (CR, CR))
            row_i = jnp.broadcast_to(g_t[hh:hh + 1, :], (CR, CR))
            log_d = jnp.where(causal_sq, jnp.broadcast_to(cf_col, (CR, CR)) - row_f + row_i, -jnp.inf)
            w_prev = cf_col + ms_rows[:, hh * HEAD_W:hh * HEAD_W + 1]
            m_t = jnp.maximum(w_prev, jnp.max(log_d, axis=1, keepdims=True))
            d_blocks.append(jnp.exp(log_d - m_t))
            mt_cols.append(m_t)
            wp_cols.append(w_prev)
            cf_cols.append(cf_col)
            ig_cols.append(ig_col)
        d_mat = jnp.concatenate(d_blocks, axis=1)
        mt = _lanes_per_head(mt_cols)
        s_prev = jnp.exp(_lanes_per_head(wp_cols) - mt)
        cumf = _lanes_per_head(cf_cols)
        lig = _lanes_per_head(ig_cols)

        qb = q.astype(BF16)
        scores = _dot_nt(qb, _stack_heads(k.astype(BF16), hm256)) * d_mat
        num = _dot(scores.astype(BF16), _stack_heads(v, hm256))
        den = _dot_sel(scores, head_expand)

        w_k = jnp.exp(seg_last_rows(cumf) - cumf + lig - seg_last_rows(mt))
        if padded:
            w_k = jnp.where(valid256, w_k, 0.0)
        kw = w_k * k
        kwb = kw.astype(BF16)
        qc, qn = [], []
        for sg in range(nseg):
            b = ch * nseg + sg
            sl = slice(sg * T, (sg + 1) * T)
            cs = stmc_s[b]
            ns = stn_s[pl.ds(b, 1), :]
            ms_b = stm_s[pl.ds(b, 1), :]
            qc.append(_dot(qb[sl], cs.astype(BF16)))
            qn.append(_dot_sel(q[sl] * ns, bd_ones))
            lr = sg * T + last
            m_new = mt[lr:lr + 1, :]
            carry = jnp.exp(cumf[lr:lr + 1, :] + ms_b - m_new)
            stmc_s[b] = carry * cs + jnp.where(bd_mask, _dot_tn(kwb[sl], v[sl]), 0.0)
            stn_s[pl.ds(b, 1), :] = carry * ns + jnp.sum(kw[sl], axis=0, keepdims=True)
            stm_s[pl.ds(b, 1), :] = m_new
        qc = qc[0] if nseg == 1 else jnp.concatenate(qc, axis=0)
        qn = qn[0] if nseg == 1 else jnp.concatenate(qn, axis=0)
        num = s_prev * qc + num
        den = s_prev * qn + den
        hcell = num / jnp.maximum(jnp.abs(den), jnp.exp(-mt))
        mix_s[rows, 512:768] = (head_norm(_sigmoid(og) * hcell) * _silu(gate)).astype(BF16)

    for_chunks(mlstm_chunk)

    gam_st = _ret_log_decay(_iota((CR, STACK), 1) >> 7)
    d_ret = jnp.where(causal_st, jnp.exp(((t_st - s_st) & (T - 1)).astype(F32) * gam_st), 0.0)
    gam128 = _ret_log_decay(_iota((CR, 128), 1) >> 5)
    gam256 = _ret_log_decay(_iota((1, 256), 1) >> 6)
    q_dec = jnp.exp((tl128 + 1).astype(F32) * gam128)
    k_dec = jnp.where(valid128, jnp.exp((last - tl128).astype(F32) * gam128), 0.0)
    s_dec = jnp.exp(float(last + 1) * gam256)
    first_half = (_iota((CR, 128), 1) & (RET_DK - 1)) < (RET_DK // 2)

    def rotary(t, cos, sin_signed):
        partner = jnp.where(first_half, pltpu.roll(t, 128 - RET_DK // 2, axis=1), pltpu.roll(t, RET_DK // 2, axis=1))
        return t * cos + partner * sin_signed

    def ret_chunk(ch):
        rows = chunk_rows(ch)
        cos = jnp.concatenate([cos_ref[...]] * nseg, axis=0)
        sin_signed = jnp.concatenate([sin_ref[...]] * nseg, axis=0)
        q = rotary(pj_s[rows, C_RT:C_RT + 128], cos, sin_signed)
        k = rotary(pj_s[rows, C_RT + 128:C_RT + 256], cos, sin_signed) * (RET_DK ** -0.5)
        v = pj_s[rows, C_RT + 256:C_RT + 512].astype(BF16)
        gate = pj_s[rows, C_RT + 512:C_RT + 768]
        scores = _dot_nt(q.astype(BF16), _stack_heads(k.astype(BF16), hm128)) * d_ret
        o = _dot(scores.astype(BF16), _stack_heads(v, hm256))
        qd = (q * q_dec).astype(BF16)
        kd = (k * k_dec).astype(BF16)
        o_inter = []
        for sg in range(nseg):
            b = ch * nseg + sg
            sl = slice(sg * T, (sg + 1) * T)
            st = strt_s[b]
            o_inter.append(_dot(qd[sl], st.astype(BF16)))
            strt_s[b] = st * s_dec + jnp.where(rt_mask, _dot_tn(kd[sl], v[sl]), 0.0)
        o = o + (o_inter[0] if nseg == 1 else jnp.concatenate(o_inter, axis=0))
        mix_s[rows, 768:1024] = (head_norm(o) * _silu(gate)).astype(BF16)

    for_chunks(ret_chunk)

    fill(len(fillers))
    out = x_ref[...].reshape(R, D_MODEL) + _dot(mix_s[...], wo_ref[...])
    if final_norm:
        ms2 = jnp.mean(out * out, axis=-1, keepdims=True)
        out = out * lax.rsqrt(ms2 + EPS) * fnw_ref[...]
    y_ref[...] = out.reshape(NB, T, D_MODEL)

    @pl.when(j == nj - 1)
    def _():
        s5re_out[...] = hre_s[...]
        s5im_out[...] = him_s[...]
        mn_out[...] = stn_s[...]
        mm_out[...] = jnp.concatenate([stm_s[:, hh * HEAD_W:hh * HEAD_W + 1] for hh in range(N_HEADS)], axis=1)
        for b in range(NB):
            hg_out[b] = _untile_state(jnp.transpose(sthg_s[b])).reshape(N_HEADS, HEAD_W, HEAD_W)
            mc_out[b] = _untile_state(stmc_s[b]).reshape(N_HEADS, HEAD_W, HEAD_W)
            rt_out[b] = _untile_state(strt_s[b]).reshape(N_HEADS, RET_DK, HEAD_W)

    if skew:
        pj_s[...] = pjn_s[...]


def _run_layer(l, x, cos_tab, sin_tab, states, prev_out, weights, *, NB, T, nvalid):
    B, L, _ = x.shape
    nbb, ntc = B // NB, L // T
    R = NB * T

    def layer_spec(shape, first, single_buffer=False):
        rest = tuple(shape[2:])
        zeros = (0,) * len(rest)
        kw = {"pipeline_mode": pl.Buffered(1)} if single_buffer else {}
        if first is None:
            return pl.BlockSpec((None, shape[1]) + rest, lambda i, j: (l, 0) + zeros, **kw)
        return pl.BlockSpec((None, first) + rest, lambda i, j: (l, i) + zeros, **kw)

    state_specs = [layer_spec(s.shape, NB) for s in states]
    in_specs = [
        pl.BlockSpec((NB, T, D_MODEL), lambda i, j: (i, j, 0)),
        pl.BlockSpec((NB, T, D_MODEL), lambda i, j: (i, jnp.minimum(j + 1, ntc - 1), 0)),
        pl.BlockSpec((T, 128), lambda i, j: (j, 0)),
        pl.BlockSpec((T, 128), lambda i, j: (j, 0)),
    ] + state_specs + [layer_spec(w.shape, None, single_buffer=True) for w in weights]
    assert len(in_specs) == N_INPUTS
    operands = [x, x, cos_tab, sin_tab, *states, *weights]
    aliases = {}
    if prev_out is not None:
        in_specs += [pl.BlockSpec(memory_space=pl.ANY)] * N_STATES
        aliases = {N_INPUTS + k: 1 + k for k in range(N_STATES)}
        operands += list(prev_out)
    out_shape = [jax.ShapeDtypeStruct(x.shape, F32)] + [jax.ShapeDtypeStruct(s.shape, F32) for s in states]
    out_specs = [pl.BlockSpec((NB, T, D_MODEL), lambda i, j: (i, j, 0))] + state_specs
    scratch = [
        pltpu.VMEM((R, W_ALL_COLS), F32),
        pltpu.VMEM((R, W_ALL_COLS) if ntc > 1 else (8, 128), F32),
        pltpu.VMEM((R, D_MODEL) if ntc > 1 else (16, 128), BF16),
        pltpu.VMEM((2 * S5_TILES, R, 128), F32),
        pltpu.VMEM((R, D_MODEL), BF16),
        pltpu.VMEM((NB, S5_N), F32),
        pltpu.VMEM((NB, S5_N), F32),
        pltpu.VMEM((NB, 256, 256), F32),
        pltpu.VMEM((NB, 256, 256), F32),
        pltpu.VMEM((NB, RET_KW, 256), F32),
        pltpu.VMEM((NB, 256), F32),
        pltpu.VMEM((NB, 256), F32),
        pltpu.VMEM((R, R), BF16),
        pltpu.VMEM((R, R), BF16),
        pltpu.VMEM((R // CHUNK_ROWS, CHUNK_ROWS, STACK), F32),
    ]
    kern = functools.partial(_layer_kernel, NB=NB, T=T, nvalid=nvalid, final_norm=(l == DEPTH - 1),
                             n_alias=len(aliases), skew=ntc > 1)
    outs = pl.pallas_call(
        kern,
        grid=(nbb, ntc),
        in_specs=in_specs,
        out_specs=out_specs,
        out_shape=out_shape,
        scratch_shapes=scratch,
        input_output_aliases=aliases,
        compiler_params=pltpu.CompilerParams(
            dimension_semantics=("arbitrary", "arbitrary"),
            vmem_limit_bytes=VMEM_LIMIT_BYTES,
        ),
    )(*operands)
    return outs[0], outs[1:]


def _rotary_tables(pos):
    half = RET_DK // 2
    inv_freq = ROPE_BASE ** (-jnp.arange(half, dtype=F32) / half)
    ang = pos.astype(F32)[:, None] * inv_freq[None, :]
    cos, sin = jnp.cos(ang), jnp.sin(ang)
    cos_t = jnp.tile(cos, (1, 128 // half))
    sin_t = jnp.tile(jnp.concatenate([-sin, sin], axis=1), (1, 128 // RET_DK))
    return cos_t, sin_t


def _block_diag(blocks):
    d, g, a, b = blocks.shape
    eye = jnp.eye(g, dtype=blocks.dtype)
    return jnp.einsum('dgab,gh->dgahb', blocks, eye).reshape(d, g * a, g * b)


def _prepare_weights(norm_w, w_in, w_out, final_norm_w, lam_re, lam_im, b_re, b_im, c_re, c_im,
                     s5_d, s5_log_dt, s5_w_glu, hgrn_lb_logits, mlstm_b_i, mlstm_b_f):
    depth = w_in.shape[0]
    w_all = jnp.concatenate(
        [w_in[:, :, 0:2560], w_in[:, :, 2568:3592], w_in[:, :, 2560:2568],
         jnp.zeros((depth, D_MODEL, W_ALL_COLS - 3592), w_in.dtype)], axis=2).astype(BF16)
    lam = lax.complex(lam_re.astype(F32), lam_im.astype(F32))
    dt = jnp.exp(s5_log_dt.astype(F32))[:, :, None]
    a_bar = jnp.exp(lam * dt)
    b_bar = ((a_bar - 1.0) / lam)[..., None] * lax.complex(b_re.astype(F32), b_im.astype(F32))
    bt = jnp.swapaxes(b_bar, 2, 3)
    bcat = jnp.concatenate([_block_diag(jnp.real(bt)), _block_diag(jnp.imag(bt))], axis=2).astype(BF16)
    ct_re = jnp.swapaxes(c_re.astype(F32), 2, 3)
    ct_im = jnp.swapaxes(c_im.astype(F32), 2, 3)
    ccat = jnp.concatenate([_block_diag(ct_re), -_block_diag(ct_im)], axis=1).astype(BF16)
    lb = jnp.cumsum(jax.nn.softmax(hgrn_lb_logits.astype(F32), axis=0), axis=0)
    lb = lb - lb[0]
    hlb = jnp.stack([jnp.log(lb), jnp.log1p(-lb), 1.0 - lb] + [jnp.zeros_like(lb)] * 5, axis=1)
    mlb = jnp.concatenate([mlstm_b_i.astype(F32), mlstm_b_f.astype(F32), jnp.zeros((depth, 120), F32)], axis=1)
    return (
        norm_w.astype(F32)[:, None, :], w_all, w_out.astype(BF16),
        jnp.broadcast_to(final_norm_w.astype(F32)[None, None, :], (depth, 1, D_MODEL)),
        jnp.real(a_bar).reshape(depth, 1, S5_N), jnp.imag(a_bar).reshape(depth, 1, S5_N), bcat, ccat,
        s5_d.astype(F32)[:, None, :], s5_w_glu.astype(BF16), hlb, mlb[:, None, :],
    )


def _trunk(x, pos, states, weights, *, NB, T, nvalid):
    s5re, s5im, hg, mc, mn, mm, rt = states
    d, b = s5re.shape[0], s5re.shape[1]
    kstates = (s5re.reshape(d, b, S5_N), s5im.reshape(d, b, S5_N), hg, mc, mn.reshape(d, b, 256), mm, rt)
    cos_tab, sin_tab = _rotary_tables(pos)
    out = None
    for l in range(DEPTH):
        x, out = _run_layer(l, x, cos_tab, sin_tab, kstates, out, weights, NB=NB, T=T, nvalid=nvalid)
    o_s5re, o_s5im, o_hg, o_mc, o_mn, o_mm, o_rt = out
    return x, [
        o_s5re.reshape(d, b, S5_GROUPS, S5_STATE), o_s5im.reshape(d, b, S5_GROUPS, S5_STATE),
        o_hg, o_mc, o_mn.reshape(d, b, N_HEADS, HEAD_W), o_mm, o_rt,
    ]


def _prompt_trunk(x, states, weights):
    b, l, _ = x.shape
    return _trunk(x, jnp.arange(l, dtype=jnp.int32), states, weights, NB=b, T=PROMPT_T, nvalid=PROMPT_T)


def _sample_trunk(x, states, weights):
    ls = x.shape[1]
    x_pad = jnp.pad(x, ((0, 0), (0, SAMPLE_T - ls), (0, 0)))
    y, new = _trunk(x_pad, PAST_LEN + jnp.arange(SAMPLE_T, dtype=jnp.int32), states, weights,
                    NB=CHUNK_ROWS // SAMPLE_T, T=SAMPLE_T, nvalid=ls)
    return y[:, :ls], new


def kernel(x_prompt, x_sample, state_s5_re, state_s5_im, state_hgrn, state_mlstm_C, state_mlstm_n,
           state_mlstm_m, state_ret, norm_w, w_in, w_out, final_norm_w, s5_lambda_re, s5_lambda_im,
           s5_B_re, s5_B_im, s5_C_re, s5_C_im, s5_D, s5_log_dt, s5_w_glu, hgrn_lb_logits,
           mlstm_b_i, mlstm_b_f):
    weights = _prepare_weights(norm_w, w_in, w_out, final_norm_w, s5_lambda_re, s5_lambda_im, s5_B_re, s5_B_im,
                               s5_C_re, s5_C_im, s5_D, s5_log_dt, s5_w_glu, hgrn_lb_logits, mlstm_b_i, mlstm_b_f)
    bp = x_prompt.shape[0]
    zero_states = (
        jnp.zeros((DEPTH, bp, S5_GROUPS, S5_STATE), F32), jnp.zeros((DEPTH, bp, S5_GROUPS, S5_STATE), F32),
        jnp.zeros((DEPTH, bp, N_HEADS, HEAD_W, HEAD_W), F32), jnp.zeros((DEPTH, bp, N_HEADS, HEAD_W, HEAD_W), F32),
        jnp.zeros((DEPTH, bp, N_HEADS, HEAD_W), F32), jnp.zeros((DEPTH, bp, N_HEADS), F32),
        jnp.zeros((DEPTH, bp, N_HEADS, RET_DK, HEAD_W), F32),
    )
    y_prompt, p_states = _prompt_trunk(x_prompt, zero_states, weights)
    sample_states = (state_s5_re, state_s5_im, state_hgrn, state_mlstm_C, state_mlstm_n, state_mlstm_m, state_ret)
    y_sample, d_states = _sample_trunk(x_sample, sample_states, weights)
    return (y_prompt, y_sample, *p_states, *d_states)
```

```python
import functools
import math

import numpy as np
import jax
import jax.numpy as jnp
from jax import lax
from jax.experimental import pallas as pl
from jax.experimental.pallas import tpu as pltpu

F32 = jnp.float32
BF16 = jnp.bfloat16

D_MODEL = 1024
DEPTH = 4
BRANCH_W = 256
N_HEADS = 4
HEAD_W = 64
S5_GROUPS = 16
S5_CH = 16
S5_STATE = 64
S5_N = S5_GROUPS * S5_STATE
S5_TILES = S5_N // 128
RET_DK = 32
RET_KW = N_HEADS * RET_DK
ROPE_BASE = 10000.0
EPS = 1e-6
LOGI_PAD = -1e30
PAST_LEN = 16384

CHUNK_ROWS = 128
STACK = N_HEADS * CHUNK_ROWS
PROMPT_T = 64
SAMPLE_T = 8
VMEM_LIMIT_BYTES = 60 * 1024 * 1024

C_S5 = 0
C_HG = 512
C_ML = 1536
C_RT = 2816
C_GATE = 3584
W_ALL_COLS = 3712
HGRN_SAFE_DECAY = 80.0


def _dot(a, b):
    return jnp.dot(a, b, preferred_element_type=F32)


def _dot_nt(a, b):
    return lax.dot_general(a, b, (((1,), (1,)), ((), ())), preferred_element_type=F32)


def _dot_tn(a, b):
    return lax.dot_general(a, b, (((0,), (0,)), ((), ())), preferred_element_type=F32)


def _split3(x):
    hi = x.astype(BF16)
    r = x - hi.astype(F32)
    mid = r.astype(BF16)
    lo = (r - mid.astype(F32)).astype(BF16)
    return hi, mid, lo


def _sel_dot(sel_bf, x):
    hi, mid, lo = _split3(x)
    return _dot(sel_bf, hi) + _dot(sel_bf, mid) + _dot(sel_bf, lo)


def _dot_sel(x, sel_bf):
    hi = x.astype(BF16)
    lo = (x - hi.astype(F32)).astype(BF16)
    return _dot(hi, sel_bf) + _dot(lo, sel_bf)


def _sigmoid(x):
    return 0.5 + 0.5 * jnp.tanh(0.5 * x)


def _silu(x):
    return x * _sigmoid(x)


def _softplus_neg_abs(x):
    return jnp.log(1.0 + jnp.exp(-jnp.abs(x)))


def _log_sigmoid(x):
    return jnp.minimum(x, 0.0) - _softplus_neg_abs(x)


def _gelu_tanh(x):
    return 0.5 * x * (1.0 + jnp.tanh(math.sqrt(2.0 / math.pi) * (x + 0.044715 * (x * x * x))))


def _iota(shape, dim):
    return lax.broadcasted_iota(jnp.int32, shape, dim)


def _log2(n):
    k = int(round(math.log2(n)))
    assert (1 << k) == n, n
    return k


def _stack_heads(x_bf, head_mask):
    return jnp.where(head_mask, jnp.concatenate([x_bf] * N_HEADS, axis=0), jnp.zeros((), x_bf.dtype))


def _tile_state(s, mask):
    return jnp.where(mask, jnp.concatenate([s] * N_HEADS, axis=1), 0.0)


def _untile_state(s_bd):
    return (s_bd[:, 0:64] + s_bd[:, 64:128]) + (s_bd[:, 128:192] + s_bd[:, 192:256])


def _lanes_per_head(cols):
    return jnp.concatenate([jnp.broadcast_to(c, (CHUNK_ROWS, HEAD_W)) for c in cols], axis=1)


def _ret_log_decay(head):
    out = jnp.full(head.shape, math.log(1.0 - 2.0 ** -5.0), F32)
    for hh in range(1, N_HEADS):
        out = jnp.where(head == hh, math.log(1.0 - 2.0 ** (-5.0 - hh)), out)
    return out


N_STATES = 7
N_INPUTS = 23
N_SAMPLE_INPUTS = 22
SAMPLE_LANES = 128


def _layer_kernel(*refs, NB, T, nvalid, final_norm, n_alias, skew):
    (x_ref, xn_ref, cos_ref, sin_ref, s5re_in, s5im_in, hg_in, mc_in, mn_in, mm_in, rt_in,
     nw_ref, w_ref, wo_ref, fnw_ref, are_ref, aim_ref, bcat_ref, ccat_ref, dsk_ref,
     wglu_ref, hlb_ref, mlb_ref) = refs[:N_INPUTS]
    (y_ref, s5re_out, s5im_out, hg_out, mc_out, mn_out, mm_out, rt_out,
     pj_s, pjn_s, hn_s, bu_s, mix_s, hre_s, him_s, sthg_s, stmc_s, strt_s, stn_s, stm_s,
     perm_s, permt_s, sc_s) = refs[N_INPUTS + n_alias:]
    R = NB * T
    NCH = R // CHUNK_ROWS
    nseg = CHUNK_ROWS // T
    last = nvalid - 1
    padded = nvalid < T
    lgT = _log2(T)
    lgNB = _log2(NB)
    CR = CHUNK_ROWS
    j = pl.program_id(1)
    nj = pl.num_programs(1)

    bd_mask = (_iota((256, 256), 0) >> 6) == (_iota((256, 256), 1) >> 6)
    rt_mask = (_iota((RET_KW, 256), 0) >> 5) == (_iota((RET_KW, 256), 1) >> 6)
    hm256 = (_iota((STACK, 256), 0) >> 7) == (_iota((STACK, 256), 1) >> 6)
    hm128 = (_iota((STACK, RET_KW), 0) >> 7) == (_iota((STACK, RET_KW), 1) >> 5)
    bd_ones = bd_mask.astype(BF16)
    head_expand = hm256.astype(BF16)

    t_st = _iota((CR, STACK), 0)
    s_st = _iota((CR, STACK), 1) & (CR - 1)
    causal_st = ((t_st >> lgT) == (s_st >> lgT)) & (t_st >= s_st)
    diag_st = t_st == s_st
    t_sq = _iota((CR, CR), 0)
    s_sq = _iota((CR, CR), 1)
    causal_sq = ((t_sq >> lgT) == (s_sq >> lgT)) & (t_sq >= s_sq)
    triseg = causal_sq.astype(BF16)
    tl256 = _iota((CR, 256), 0) & (T - 1)
    tl128 = _iota((CR, 128), 0) & (T - 1)
    row256 = _iota((CR, 256), 0)
    valid256 = tl256 <= last
    valid128 = tl128 <= last

    @pl.when(j == 0)
    def _():
        ri = _iota((R, R), 0)
        ci = _iota((R, R), 1)
        perm_s[...] = (ci == (((ri & (NB - 1)) << lgT) + (ri >> lgNB))).astype(BF16)
        permt_s[...] = (ri == (((ci & (NB - 1)) << lgT) + (ci >> lgNB))).astype(BF16)
        hre_s[...] = s5re_in[...]
        him_s[...] = s5im_in[...]
        stn_s[...] = mn_in[...]
        m_in = mm_in[...]
        stm_s[...] = jnp.concatenate(
            [jnp.broadcast_to(m_in[:, hh:hh + 1], (NB, HEAD_W)) for hh in range(N_HEADS)], axis=1)
        for b in range(NB):
            sthg_s[b] = jnp.transpose(_tile_state(hg_in[b].reshape(256, HEAD_W), bd_mask))
            stmc_s[b] = _tile_state(mc_in[b].reshape(256, HEAD_W), bd_mask)
            strt_s[b] = _tile_state(rt_in[b].reshape(RET_KW, HEAD_W), rt_mask)

    def project(src_ref, dst_ref):
        xs = src_ref[...].reshape(R, D_MODEL)
        ms = jnp.mean(xs * xs, axis=-1, keepdims=True)
        hn = (xs * lax.rsqrt(ms + EPS) * nw_ref[...]).astype(BF16)
        for i in range(C_GATE // 256):
            dst_ref[:, 256 * i:256 * (i + 1)] = _dot_nt(hn, w_ref[256 * i:256 * (i + 1), :])
        dst_ref[:, C_GATE:W_ALL_COLS] = _dot_nt(hn, w_ref[C_GATE:W_ALL_COLS, :])

    @pl.when(j == 0)
    def _():
        project(x_ref, pj_s)

    fillers = []
    if skew:
        xs = xn_ref[...].reshape(R, D_MODEL)
        ms_n = jnp.mean(xs * xs, axis=-1, keepdims=True)
        hn_s[...] = (xs * lax.rsqrt(ms_n + EPS) * nw_ref[...]).astype(BF16)

        def proj_tile(c0, c1):
            def task():
                pjn_s[:, c0:c1] = _dot_nt(hn_s[...], w_ref[c0:c1, :])
            return task

        fillers = [proj_tile(256 * i, 256 * (i + 1)) for i in range(C_GATE // 256)]
        fillers.append(proj_tile(C_GATE, W_ALL_COLS))

    def fill(n=1):
        for _ in range(min(n, len(fillers))):
            fillers.pop(0)()

    def head_norm(o):
        mu = _dot_sel(o, bd_ones) * (1.0 / HEAD_W)
        d = o - mu
        var = _dot((d * d).astype(BF16), bd_ones) * (1.0 / HEAD_W)
        return d * lax.rsqrt(var + EPS)

    def for_chunks(fn):
        for ch in range(NCH):
            fill()
            fn(ch)

    def seg_last_rows(a):
        w = a.shape[1]
        pieces = [jnp.broadcast_to(a[sg * T + last:sg * T + last + 1, :], (T, w)) for sg in range(nseg)]
        return pieces[0] if nseg == 1 else jnp.concatenate(pieces, axis=0)

    def chunk_rows(ch):
        if isinstance(ch, int):
            return pl.ds(ch * CR, CR)
        return pl.ds(pl.multiple_of(ch * CR, CR), CR)

    u = pj_s[:, C_S5:C_S5 + 256]
    ub = _dot(perm_s[...], u.astype(BF16)).astype(BF16)
    for i in range(S5_TILES):
        drive = _dot(ub, bcat_ref[:, 256 * i:256 * (i + 1)])
        bu_s[2 * i] = drive[:, 0:128]
        bu_s[2 * i + 1] = drive[:, 128:256]
    a_re = are_ref[...]
    a_im = aim_ref[...]

    def s5_step(t, carry):
        idx = pl.ds(t * NB, NB)
        new = []
        for c in range(S5_TILES):
            hre, him = carry[2 * c], carry[2 * c + 1]
            ar = a_re[:, 128 * c:128 * (c + 1)]
            ai = a_im[:, 128 * c:128 * (c + 1)]
            nre = ar * hre - ai * him + bu_s[c, idx, :]
            nim = ar * him + ai * hre + bu_s[S5_TILES + c, idx, :]
            bu_s[c, idx, :] = nre
            bu_s[S5_TILES + c, idx, :] = nim
            new += [nre, nim]
        return tuple(new)

    init = []
    for c in range(S5_TILES):
        init += [hre_s[:, 128 * c:128 * (c + 1)], him_s[:, 128 * c:128 * (c + 1)]]
    fill()
    fin = tuple(init)
    for t in range(nvalid):
        fin = s5_step(t, fin)
    for c in range(S5_TILES):
        hre_s[:, 128 * c:128 * (c + 1)] = fin[2 * c]
        him_s[:, 128 * c:128 * (c + 1)] = fin[2 * c + 1]
    hid = jnp.concatenate([bu_s[c].astype(BF16) for c in range(2 * S5_TILES)], axis=1)
    y = _sel_dot(permt_s[...], _dot(hid, ccat_ref[...])) + dsk_ref[...] * u
    fill()
    y = _gelu_tanh(y)
    y = y * _sigmoid(_dot(y.astype(BF16), wglu_ref[...]))
    mix_s[:, 0:256] = (y * _silu(pj_s[:, C_S5 + 256:C_S5 + 512])).astype(BF16)

    log_lb = hlb_ref[0:1, :]
    log1m_lb = hlb_ref[1:2, :]
    one_m_lb = hlb_ref[2:3, :]

    def boundary_rows(c, m):
        blk = 2 * m
        if blk >= 8:
            c3 = c.reshape(CR // blk, blk, 256)
            return jnp.broadcast_to(c3[:, m - 1:m, :], (CR // blk, blk, 256)).reshape(CR, 256)
        c3 = c.reshape(CR // 8, 8, 256)
        sub = _iota((CR // 8, 8, 256), 1)
        out = None
        for jb in range(8 // blk):
            piece = jnp.broadcast_to(c3[:, jb * blk + m - 1:jb * blk + m, :], (CR // 8, 8, 256))
            out = piece if out is None else jnp.where((sub >> _log2(blk)) == jb, piece, out)
        return out.reshape(CR, 256)

    def hgrn_gates(rows):
        q = _silu(pj_s[rows, C_HG:C_HG + 256])
        xf = pj_s[rows, C_HG + 256:C_HG + 512]
        b_ = log1m_lb + _log_sigmoid(xf)
        log_f = jnp.maximum(log_lb, b_) + _softplus_neg_abs(log_lb - b_)
        nlf = -log_f
        hk = one_m_lb * _sigmoid(-xf)
        if padded:
            nlf = jnp.where(valid256, nlf, 0.0)
            hk = jnp.where(valid256, hk, 0.0)
        return q, hk, _sel_dot(triseg, nlf)

    def hgrn_scores_any_decay(q, hk, c):
        scores = jnp.where(diag_st, _dot_nt(q.astype(BF16), _stack_heads(hk.astype(BF16), hm256)), 0.0)
        m = T // 2
        while m >= 1:
            e = jnp.exp(-jnp.abs(c - boundary_rows(c, m)))
            second = (row256 & (2 * m - 1)) >= m
            qe = jnp.where(second, q * e, 0.0).astype(BF16)
            ke = jnp.where(second, 0.0, hk * e).astype(BF16)
            lg = _log2(2 * m)
            same_blk = (t_st >> lg) == (s_st >> lg)
            scores = scores + jnp.where(same_blk, _dot_nt(qe, _stack_heads(ke, hm256)), 0.0)
            m //= 2
        return scores

    fill()
    gates = [hgrn_gates(chunk_rows(ch)) for ch in range(NCH)]
    c_max = gates[0][2]
    for ch in range(1, NCH):
        c_max = jnp.maximum(c_max, gates[ch][2])
    c_max = jnp.max(c_max)
    for ch in range(NCH):
        q, hk, c = gates[ch]
        ke = _stack_heads((hk * jnp.exp(c)).astype(BF16), hm256)
        sc_s[ch] = jnp.where(causal_st, _dot_nt((q * jnp.exp(-c)).astype(BF16), ke), 0.0)

    @pl.when(c_max > HGRN_SAFE_DECAY)
    def _():
        def redo(ch, carry):
            q, hk, c = hgrn_gates(chunk_rows(ch))
            sc_s[ch] = hgrn_scores_any_decay(q, hk, c)
            return carry
        lax.fori_loop(0, NCH, redo, 0)

    def hgrn_chunk(ch):
        rows = chunk_rows(ch)
        q, hk, c = gates[ch]
        v = pj_s[rows, C_HG + 512:C_HG + 768].astype(BF16)
        gate = pj_s[rows, C_HG + 768:C_HG + 1024]
        o = _dot(sc_s[ch].astype(BF16), _stack_heads(v, hm256))

        qd = (q * jnp.exp(-c)).astype(BF16)
        kd = hk * jnp.exp(-(seg_last_rows(c) - c))
        if padded:
            kd = jnp.where(valid256, kd, 0.0)
        kd = kd.astype(BF16)
        o_inter = []
        for sg in range(nseg):
            b = ch * nseg + sg
            sl = slice(sg * T, (sg + 1) * T)
            st = sthg_s[b]
            o_inter.append(_dot_nt(qd[sl], st.astype(BF16)))
            c_last = c[sg * T + last:sg * T + last + 1, :]
            upd = _dot_tn(v[sl], kd[sl])
            sthg_s[b] = st * jnp.exp(-c_last) + jnp.where(bd_mask, upd, 0.0)
        o = o + (o_inter[0] if nseg == 1 else jnp.concatenate(o_inter, axis=0))
        mix_s[rows, 256:512] = (head_norm(o) * _silu(gate)).astype(BF16)

    for_chunks(hgrn_chunk)

    lane128 = _iota((CR, 128), 1)
    is_i = lane128 < N_HEADS
    is_f = (lane128 >= N_HEADS) & (lane128 < 2 * N_HEADS)

    def mlstm_chunk(ch):
        rows = chunk_rows(ch)
        q = pj_s[rows, C_ML:C_ML + 256]
        k = pj_s[rows, C_ML + 256:C_ML + 512] * (HEAD_W ** -0.5)
        v = pj_s[rows, C_ML + 512:C_ML + 768].astype(BF16)
        og = pj_s[rows, C_ML + 768:C_ML + 1024]
        gate = pj_s[rows, C_ML + 1024:C_ML + 1280]
        gpre = pj_s[rows, C_GATE:W_ALL_COLS] + mlb_ref[...]
        log_i = gpre
        log_fg = _log_sigmoid(gpre)
        if padded:
            log_i = jnp.where(valid128, log_i, LOGI_PAD)
            log_fg = jnp.where(valid128, log_fg, 0.0)
        cum = _sel_dot(triseg, jnp.where(is_f, log_fg, 0.0))
        g = jnp.where(is_i, log_i, cum)
        g_t = jnp.transpose(g)

        if nseg == 1:
            ms_rows = jnp.broadcast_to(stm_s[pl.ds(ch, 1), :], (CR, 256))
        else:
            ms_rows = jnp.concatenate(
                [jnp.broadcast_to(stm_s[pl.ds(ch * nseg + sg, 1), :], (T, 256)) for sg in range(nseg)], axis=0)

        d_blocks, mt_cols, wp_cols, cf_cols, ig_cols = [], [], [], [], []
        for hh in range(N_HEADS):
            cf_col = g[:, N_HEADS + hh:N_HEADS + hh + 1]
            ig_col = g[:, hh:hh + 1]
            row_f = jnp.broadcast_to(g_t[N_HEADS + hh:N_HEADS + hh + 1, :], (CR, CR))
            row_i = jnp.broadcast_to(g_t[hh:hh + 1, :], (CR, CR))
            log_d = jnp.where(causal_sq, jnp.broadcast_to(cf_col, (CR, CR)) - row_f + row_i, -jnp.inf)
            w_prev = cf_col + ms_rows[:, hh * HEAD_W:hh * HEAD_W + 1]
            m_t = jnp.maximum(w_prev, jnp.max(log_d, axis=1, keepdims=True))
            d_blocks.append(jnp.exp(log_d - m_t))
            mt_cols.append(m_t)
            wp_cols.append(w_prev)
            cf_cols.append(cf_col)
            ig_cols.append(ig_col)
        d_mat = jnp.concatenate(d_blocks, axis=1)
        mt = _lanes_per_head(mt_cols)
        s_prev = jnp.exp(_lanes_per_head(wp_cols) - mt)
        cumf = _lanes_per_head(cf_cols)
        lig = _lanes_per_head(ig_cols)

        qb = q.astype(BF16)
        scores = _dot_nt(qb, _stack_heads(k.astype(BF16), hm256)) * d_mat
        num = _dot(scores.astype(BF16), _stack_heads(v, hm256))
        den = _dot_sel(scores, head_expand)

        w_k = jnp.exp(seg_last_rows(cumf) - cumf + lig - seg_last_rows(mt))
        if padded:
            w_k = jnp.where(valid256, w_k, 0.0)
        kw = w_k * k
        kwb = kw.astype(BF16)
        qc, qn = [], []
        for sg in range(nseg):
            b = ch * nseg + sg
            sl = slice(sg * T, (sg + 1) * T)
            cs = stmc_s[b]
            ns = stn_s[pl.ds(b, 1), :]
            ms_b = stm_s[pl.ds(b, 1), :]
            qc.append(_dot(qb[sl], cs.astype(BF16)))
            qn.append(_dot_sel(q[sl] * ns, bd_ones))
            lr = sg * T + last
            m_new = mt[lr:lr + 1, :]
            carry = jnp.exp(cumf[lr:lr + 1, :] + ms_b - m_new)
            stmc_s[b] = carry * cs + jnp.where(bd_mask, _dot_tn(kwb[sl], v[sl]), 0.0)
            stn_s[pl.ds(b, 1), :] = carry * ns + jnp.sum(kw[sl], axis=0, keepdims=True)
            stm_s[pl.ds(b, 1), :] = m_new
        qc = qc[0] if nseg == 1 else jnp.concatenate(qc, axis=0)
        qn = qn[0] if nseg == 1 else jnp.concatenate(qn, axis=0)
        num = s_prev * qc + num
        den = s_prev * qn + den
        hcell = num / jnp.maximum(jnp.abs(den), jnp.exp(-mt))
        mix_s[rows, 512:768] = (head_norm(_sigmoid(og) * hcell) * _silu(gate)).astype(BF16)

    for_chunks(mlstm_chunk)

    gam_st = _ret_log_decay(_iota((CR, STACK), 1) >> 7)
    d_ret = jnp.where(causal_st, jnp.exp(((t_st - s_st) & (T - 1)).astype(F32) * gam_st), 0.0)
    gam128 = _ret_log_decay(_iota((CR, 128), 1) >> 5)
    gam256 = _ret_log_decay(_iota((1, 256), 1) >> 6)
    q_dec = jnp.exp((tl128 + 1).astype(F32) * gam128)
    k_dec = jnp.where(valid128, jnp.exp((last - tl128).astype(F32) * gam128), 0.0)
    s_dec = jnp.exp(float(last + 1) * gam256)
    first_half = (_iota((CR, 128), 1) & (RET_DK - 1)) < (RET_DK // 2)

    def rotary(t, cos, sin_signed):
        partner = jnp.where(first_half, pltpu.roll(t, 128 - RET_DK // 2, axis=1), pltpu.roll(t, RET_DK // 2, axis=1))
        return t * cos + partner * sin_signed

    def ret_chunk(ch):
        rows = chunk_rows(ch)
        cos = jnp.concatenate([cos_ref[...]] * nseg, axis=0)
        sin_signed = jnp.concatenate([sin_ref[...]] * nseg, axis=0)
        q = rotary(pj_s[rows, C_RT:C_RT + 128], cos, sin_signed)
        k = rotary(pj_s[rows, C_RT + 128:C_RT + 256], cos, sin_signed) * (RET_DK ** -0.5)
        v = pj_s[rows, C_RT + 256:C_RT + 512].astype(BF16)
        gate = pj_s[rows, C_RT + 512:C_RT + 768]
        scores = _dot_nt(q.astype(BF16), _stack_heads(k.astype(BF16), hm128)) * d_ret
        o = _dot(scores.astype(BF16), _stack_heads(v, hm256))
        qd = (q * q_dec).astype(BF16)
        kd = (k * k_dec).astype(BF16)
        o_inter = []
        for sg in range(nseg):
            b = ch * nseg + sg
            sl = slice(sg * T, (sg + 1) * T)
            st = strt_s[b]
            o_inter.append(_dot(qd[sl], st.astype(BF16)))
            strt_s[b] = st * s_dec + jnp.where(rt_mask, _dot_tn(kd[sl], v[sl]), 0.0)
        o = o + (o_inter[0] if nseg == 1 else jnp.concatenate(o_inter, axis=0))
        mix_s[rows, 768:1024] = (head_norm(o) * _silu(gate)).astype(BF16)

    for_chunks(ret_chunk)

    fill(len(fillers))
    out = x_ref[...].reshape(R, D_MODEL) + _dot(mix_s[...], wo_ref[...])
    if final_norm:
        ms2 = jnp.mean(out * out, axis=-1, keepdims=True)
        out = out * lax.rsqrt(ms2 + EPS) * fnw_ref[...]
    y_ref[...] = out.reshape(NB, T, D_MODEL)

    @pl.when(j == nj - 1)
    def _():
        s5re_out[...] = hre_s[...]
        s5im_out[...] = him_s[...]
        mn_out[...] = stn_s[...]
        mm_out[...] = jnp.concatenate([stm_s[:, hh * HEAD_W:hh * HEAD_W + 1] for hh in range(N_HEADS)], axis=1)
        for b in range(NB):
            hg_out[b] = _untile_state(jnp.transpose(sthg_s[b])).reshape(N_HEADS, HEAD_W, HEAD_W)
            mc_out[b] = _untile_state(stmc_s[b]).reshape(N_HEADS, HEAD_W, HEAD_W)
            rt_out[b] = _untile_state(strt_s[b]).reshape(N_HEADS, RET_DK, HEAD_W)

    if skew:
        pj_s[...] = pjn_s[...]


def _run_layer(l, x, cos_tab, sin_tab, states, prev_out, weights, *, NB, T, nvalid):
    B, L, _ = x.shape
    nbb, ntc = B // NB, L // T
    R = NB * T

    def layer_spec(shape, first, single_buffer=False):
        rest = tuple(shape[2:])
        zeros = (0,) * len(rest)
        kw = {"pipeline_mode": pl.Buffered(1)} if single_buffer else {}
        if first is None:
            return pl.BlockSpec((None, shape[1]) + rest, lambda i, j: (l, 0) + zeros, **kw)
        return pl.BlockSpec((None, first) + rest, lambda i, j: (l, i) + zeros, **kw)

    state_specs = [layer_spec(s.shape, NB) for s in states]
    in_specs = [
        pl.BlockSpec((NB, T, D_MODEL), lambda i, j: (i, j, 0)),
        pl.BlockSpec((NB, T, D_MODEL), lambda i, j: (i, jnp.minimum(j + 1, ntc - 1), 0)),
        pl.BlockSpec((T, 128), lambda i, j: (j, 0)),
        pl.BlockSpec((T, 128), lambda i, j: (j, 0)),
    ] + state_specs + [layer_spec(w.shape, None, single_buffer=True) for w in weights]
    assert len(in_specs) == N_INPUTS
    operands = [x, x, cos_tab, sin_tab, *states, *weights]
    aliases = {}
    if prev_out is not None:
        in_specs += [pl.BlockSpec(memory_space=pl.ANY)] * N_STATES
        aliases = {N_INPUTS + k: 1 + k for k in range(N_STATES)}
        operands += list(prev_out)
    out_shape = [jax.ShapeDtypeStruct(x.shape, F32)] + [jax.ShapeDtypeStruct(s.shape, F32) for s in states]
    out_specs = [pl.BlockSpec((NB, T, D_MODEL), lambda i, j: (i, j, 0))] + state_specs
    scratch = [
        pltpu.VMEM((R, W_ALL_COLS), F32),
        pltpu.VMEM((R, W_ALL_COLS) if ntc > 1 else (8, 128), F32),
        pltpu.VMEM((R, D_MODEL) if ntc > 1 else (16, 128), BF16),
        pltpu.VMEM((2 * S5_TILES, R, 128), F32),
        pltpu.VMEM((R, D_MODEL), BF16),
        pltpu.VMEM((NB, S5_N), F32),
        pltpu.VMEM((NB, S5_N), F32),
        pltpu.VMEM((NB, 256, 256), F32),
        pltpu.VMEM((NB, 256, 256), F32),
        pltpu.VMEM((NB, RET_KW, 256), F32),
        pltpu.VMEM((NB, 256), F32),
        pltpu.VMEM((NB, 256), F32),
        pltpu.VMEM((R, R), BF16),
        pltpu.VMEM((R, R), BF16),
        pltpu.VMEM((R // CHUNK_ROWS, CHUNK_ROWS, STACK), F32),
    ]
    kern = functools.partial(_layer_kernel, NB=NB, T=T, nvalid=nvalid, final_norm=(l == DEPTH - 1),
                             n_alias=len(aliases), skew=ntc > 1)
    outs = pl.pallas_call(
        kern,
        grid=(nbb, ntc),
        in_specs=in_specs,
        out_specs=out_specs,
        out_shape=out_shape,
        scratch_shapes=scratch,
        input_output_aliases=aliases,
        compiler_params=pltpu.CompilerParams(
            dimension_semantics=("arbitrary", "arbitrary"),
            vmem_limit_bytes=VMEM_LIMIT_BYTES,
        ),
    )(*operands)
    return outs[0], outs[1:]


def _sample_kernel(*refs, L, final_norm, n_alias):
    (x_ref, s5re_in, s5im_in, hg_in, mc_in, mn_in, mm_in, rt_in,
     nw_ref, wt_ref, wo_ref, fnw_ref, are_ref, aim_ref, bcatt_ref, ccatt_ref, dsk_ref, wglut_ref,
     hlb_ref, mlb_ref, cos_ref, sin_ref) = refs[:N_SAMPLE_INPUTS]
    (y_ref, s5re_out, s5im_out, hg_out, mc_out, mn_out, mm_out, rt_out,
     pj_s, bu_s, hid_s, mix_s, a_s, b_s, c_s, d_s) = refs[N_SAMPLE_INPUTS + n_alias:]
    h = pl.program_id(0)
    nh = pl.num_programs(0)
    B = SAMPLE_LANES
    half = RET_DK // 2

    def cols(t):
        return slice(t * B, (t + 1) * B)

    def tile_tokens(p):
        return jnp.concatenate([p] * L, axis=1)

    @pl.when(h == 0)
    def _():
        hs = []
        for t in range(L):
            xt = x_ref[t]
            ms = jnp.mean(xt * xt, axis=-1, keepdims=True)
            hs.append((xt * lax.rsqrt(ms + EPS) * nw_ref[...]).astype(BF16))
        hb = jnp.concatenate(hs, axis=0)
        for r0 in range(0, W_ALL_COLS, 256):
            r1 = min(r0 + 256, W_ALL_COLS)
            pj_s[r0:r1, :] = _dot_nt(wt_ref[r0:r1, :], hb)

        u = pj_s[C_S5:C_S5 + 256, :]
        bu_s[...] = _dot(bcatt_ref[...], u.astype(BF16))
        hre, him = s5re_in[...], s5im_in[...]
        a_re, a_im = are_ref[...], aim_ref[...]
        for t in range(L):
            nre = a_re * hre - a_im * him + bu_s[0:S5_N, cols(t)]
            nim = a_re * him + a_im * hre + bu_s[S5_N:2 * S5_N, cols(t)]
            hid_s[0:S5_N, cols(t)] = nre.astype(BF16)
            hid_s[S5_N:2 * S5_N, cols(t)] = nim.astype(BF16)
            hre, him = nre, nim
        s5re_out[...] = hre
        s5im_out[...] = him
        y = _dot(ccatt_ref[...], hid_s[...]) + tile_tokens(dsk_ref[...]) * u
        y = _gelu_tanh(y)
        y = y * _sigmoid(_dot(wglut_ref[...], y.astype(BF16)))
        mix_s[0:256, :] = (y * _silu(pj_s[C_S5 + 256:C_S5 + 512, :])).astype(BF16)

    def head_rows(base, width=HEAD_W):
        return pl.ds(pl.multiple_of(base + h * width, width), width)

    def head_norm_rows(o):
        mu = jnp.mean(o, axis=0, keepdims=True)
        d = o - mu
        return d * lax.rsqrt(jnp.mean(d * d, axis=0, keepdims=True) + EPS)

    def row_on_sublanes(ref, t, k):
        return jnp.broadcast_to(ref[t, pl.ds(k, 1), :], (HEAD_W, B))

    def recur(nk, st_in, st_out, decay, key_s, query_s, value_s):
        def body(k, outs):
            s = st_in[k]
            new = []
            for t in range(L):
                s = decay(t, k) * s + row_on_sublanes(key_s, t, k) * value_s[t]
                new.append(outs[t] + row_on_sublanes(query_s, t, k) * s)
            st_out[k] = s
            return tuple(new)
        return lax.fori_loop(0, nk, body, (jnp.zeros((HEAD_W, B), F32),) * L, unroll=2)

    hrows = pl.ds(pl.multiple_of(h * HEAD_W, HEAD_W), HEAD_W)
    log_lb, log1m_lb, one_m_lb = hlb_ref[0, hrows, :], hlb_ref[1, hrows, :], hlb_ref[2, hrows, :]
    for t in range(L):
        xf = pj_s[head_rows(C_HG + 256), cols(t)]
        b_ = log1m_lb + _log_sigmoid(xf)
        log_f = jnp.maximum(log_lb, b_) + _softplus_neg_abs(log_lb - b_)
        a_s[t] = jnp.exp(log_f)
        b_s[t] = one_m_lb * _sigmoid(-xf)
        c_s[t] = _silu(pj_s[head_rows(C_HG), cols(t)])
        d_s[t] = pj_s[head_rows(C_HG + 512), cols(t)]
    outs = recur(HEAD_W, hg_in, hg_out, lambda t, k: row_on_sublanes(a_s, t, k), b_s, c_s, d_s)
    for t in range(L):
        gate = pj_s[head_rows(C_HG + 768), cols(t)]
        mix_s[head_rows(256), cols(t)] = (head_norm_rows(outs[t]) * _silu(gate)).astype(BF16)

    m_prev = mm_in[pl.ds(h, 1), :]
    n_vec = mn_in[...]
    gate_row = _iota((2 * N_HEADS, B), 0)
    dens, inv_caps = [], []
    for t in range(L):
        gates = pj_s[C_GATE:C_GATE + 2 * N_HEADS, cols(t)] + mlb_ref[...]
        log_i = jnp.sum(jnp.where(gate_row == h, gates, 0.0), axis=0, keepdims=True)
        log_fg = _log_sigmoid(jnp.sum(jnp.where(gate_row == h + N_HEADS, gates, 0.0), axis=0, keepdims=True))
        m_t = jnp.maximum(log_fg + m_prev, log_i)
        f_dec = jnp.exp(log_fg + m_prev - m_t)
        q = pj_s[head_rows(C_ML), cols(t)]
        kw = jnp.exp(log_i - m_t) * (pj_s[head_rows(C_ML + 256), cols(t)] * (HEAD_W ** -0.5))
        n_vec = f_dec * n_vec + kw
        dens.append(jnp.sum(q * n_vec, axis=0, keepdims=True))
        inv_caps.append(jnp.exp(-m_t))
        a_s[t] = jnp.broadcast_to(f_dec, (HEAD_W, B))
        b_s[t] = kw
        c_s[t] = q
        d_s[t] = pj_s[head_rows(C_ML + 512), cols(t)]
        m_prev = m_t
    outs = recur(HEAD_W, mc_in, mc_out, lambda t, k: a_s[t], b_s, c_s, d_s)
    mn_out[...] = n_vec
    mm_out[pl.ds(h, 1), :] = m_prev
    for t in range(L):
        cell = outs[t] / jnp.maximum(jnp.abs(dens[t]), inv_caps[t])
        og = pj_s[head_rows(C_ML + 768), cols(t)]
        gate = pj_s[head_rows(C_ML + 1024), cols(t)]
        mix_s[head_rows(512), cols(t)] = (head_norm_rows(_sigmoid(og) * cell) * _silu(gate)).astype(BF16)

    decay = jnp.float32(1.0 - 2.0 ** -5.0)
    for hh in range(1, N_HEADS):
        decay = jnp.where(h == hh, jnp.float32(1.0 - 2.0 ** (-5.0 - hh)), decay)
    for t in range(L):
        cos, sin = cos_ref[t], sin_ref[t]
        for base, dst, scale in ((C_RT, c_s, 1.0), (C_RT + RET_KW, b_s, RET_DK ** -0.5)):
            t1 = pj_s[pl.ds(pl.multiple_of(base + h * RET_DK, half), half), cols(t)] * scale
            t2 = pj_s[pl.ds(pl.multiple_of(base + h * RET_DK + half, half), half), cols(t)] * scale
            dst[t, 0:half, :] = t1 * cos - t2 * sin
            dst[t, half:RET_DK, :] = t1 * sin + t2 * cos
        d_s[t] = pj_s[head_rows(C_RT + 256), cols(t)]
    outs = recur(RET_DK, rt_in, rt_out, lambda t, k: decay, b_s, c_s, d_s)
    for t in range(L):
        gate = pj_s[head_rows(C_RT + 512), cols(t)]
        mix_s[head_rows(768), cols(t)] = (head_norm_rows(outs[t]) * _silu(gate)).astype(BF16)

    @pl.when(h == nh - 1)
    def _():
        yn = _dot_tn(mix_s[...], wo_ref[...])
        for t in range(L):
            out = x_ref[t] + yn[t * B:(t + 1) * B, :]
            if final_norm:
                ms2 = jnp.mean(out * out, axis=-1, keepdims=True)
                out = out * lax.rsqrt(ms2 + EPS) * fnw_ref[...]
            y_ref[t] = out


def _run_sample_layer(l, x, states, prev_out, weights):
    L, B, _ = x.shape
    assert B == SAMPLE_LANES

    def lspec(shape, per_head, single_buffer=False):
        kw = {"pipeline_mode": pl.Buffered(1)} if single_buffer else {}
        if per_head == "axis":
            rest = tuple(shape[2:])
            return pl.BlockSpec((None, None) + rest, lambda h: (l, h) + (0,) * len(rest), **kw)
        if per_head == "rows":
            return pl.BlockSpec((None, HEAD_W, shape[2]), lambda h: (l, h, 0), **kw)
        rest = tuple(shape[1:])
        return pl.BlockSpec((None,) + rest, lambda h: (l,) + (0,) * len(rest), **kw)

    s5re, s5im, hg, mc, mn, mm, rt = states
    state_specs = [lspec(s5re.shape, None), lspec(s5im.shape, None), lspec(hg.shape, "axis"), lspec(mc.shape, "axis"),
                   lspec(mn.shape, "rows"), lspec(mm.shape, None), lspec(rt.shape, "axis")]
    (nw, wt, wo, fnw, are_b, aim_b, bcatt, ccatt, dsk_b, wglut, hlb_b, mlb_b, cos_t, sin_t) = weights
    table_spec = pl.BlockSpec(cos_t.shape, lambda h: (0, 0, 0))
    in_specs = ([pl.BlockSpec(x.shape, lambda h: (0, 0, 0))] + state_specs
                + [lspec(w.shape, None, single_buffer=True) for w in weights[:-2]] + [table_spec, table_spec])
    assert len(in_specs) == N_SAMPLE_INPUTS
    operands = [x, *states, *weights]
    aliases = {}
    if prev_out is not None:
        in_specs += [pl.BlockSpec(memory_space=pl.ANY)] * N_STATES
        aliases = {N_SAMPLE_INPUTS + k: 1 + k for k in range(N_STATES)}
        operands += list(prev_out)
    n_cols = L * B
    scratch = [
        pltpu.VMEM((W_ALL_COLS, n_cols), F32),
        pltpu.VMEM((2 * S5_N, n_cols), F32),
        pltpu.VMEM((2 * S5_N, n_cols), BF16),
        pltpu.VMEM((D_MODEL, n_cols), BF16),
    ] + [pltpu.VMEM((L, HEAD_W, B), F32)] * 4
    kern = functools.partial(_sample_kernel, L=L, final_norm=(l == DEPTH - 1), n_alias=len(aliases))
    outs = pl.pallas_call(
        kern,
        grid=(N_HEADS,),
        in_specs=in_specs,
        out_specs=[pl.BlockSpec(x.shape, lambda h: (0, 0, 0))] + state_specs,
        out_shape=[jax.ShapeDtypeStruct(x.shape, F32)] + [jax.ShapeDtypeStruct(s.shape, F32) for s in states],
        scratch_shapes=scratch,
        input_output_aliases=aliases,
        compiler_params=pltpu.CompilerParams(
            dimension_semantics=("arbitrary",),
            vmem_limit_bytes=VMEM_LIMIT_BYTES,
        ),
    )(*operands)
    return outs[0], outs[1:]


def _rotary_tables(pos):
    half = RET_DK // 2
    inv_freq = ROPE_BASE ** (-jnp.arange(half, dtype=F32) / half)
    ang = pos.astype(F32)[:, None] * inv_freq[None, :]
    cos, sin = jnp.cos(ang), jnp.sin(ang)
    cos_t = jnp.tile(cos, (1, 128 // half))
    sin_t = jnp.tile(jnp.concatenate([-sin, sin], axis=1), (1, 128 // RET_DK))
    return cos_t, sin_t


def _block_diag(blocks):
    d, g, a, b = blocks.shape
    eye = jnp.eye(g, dtype=blocks.dtype)
    return jnp.einsum('dgab,gh->dgahb', blocks, eye).reshape(d, g * a, g * b)


def _prepare_weights(norm_w, w_in, w_out, final_norm_w, lam_re, lam_im, b_re, b_im, c_re, c_im,
                     s5_d, s5_log_dt, s5_w_glu, hgrn_lb_logits, mlstm_b_i, mlstm_b_f):
    depth = w_in.shape[0]
    w_t = jnp.swapaxes(w_in, 1, 2)
    w_all = jnp.concatenate(
        [w_t[:, 0:2560], w_t[:, 2568:3592], w_t[:, 2560:2568],
         jnp.zeros((depth, W_ALL_COLS - 3592, D_MODEL), w_in.dtype)], axis=1).astype(BF16)
    lam = lax.complex(lam_re.astype(F32), lam_im.astype(F32))
    dt = jnp.exp(s5_log_dt.astype(F32))[:, :, None]
    a_bar = jnp.exp(lam * dt)
    b_bar = ((a_bar - 1.0) / lam)[..., None] * lax.complex(b_re.astype(F32), b_im.astype(F32))
    bt = jnp.swapaxes(b_bar, 2, 3)
    bcat = jnp.concatenate([_block_diag(jnp.real(bt)), _block_diag(jnp.imag(bt))], axis=2).astype(BF16)
    ct_re = jnp.swapaxes(c_re.astype(F32), 2, 3)
    ct_im = jnp.swapaxes(c_im.astype(F32), 2, 3)
    ccat = jnp.concatenate([_block_diag(ct_re), -_block_diag(ct_im)], axis=1).astype(BF16)
    lb = jnp.cumsum(jax.nn.softmax(hgrn_lb_logits.astype(F32), axis=0), axis=0)
    lb = lb - lb[0]
    hlb = jnp.stack([jnp.log(lb), jnp.log1p(-lb), 1.0 - lb] + [jnp.zeros_like(lb)] * 5, axis=1)
    mlb = jnp.concatenate([mlstm_b_i.astype(F32), mlstm_b_f.astype(F32), jnp.zeros((depth, 120), F32)], axis=1)
    nw = norm_w.astype(F32)[:, None, :]
    wo = w_out.astype(BF16)
    fnw = jnp.broadcast_to(final_norm_w.astype(F32)[None, None, :], (depth, 1, D_MODEL))
    a_re, a_im = jnp.real(a_bar).reshape(depth, S5_N), jnp.imag(a_bar).reshape(depth, S5_N)
    wglu = s5_w_glu.astype(BF16)
    prompt_weights = (nw, w_all, wo, fnw, a_re[:, None, :], a_im[:, None, :], bcat, ccat,
                      s5_d.astype(F32)[:, None, :], wglu, hlb, mlb[:, None, :])

    def on_lanes(v):
        return jnp.broadcast_to(v[..., None], v.shape + (SAMPLE_LANES,))
    sample_weights = (nw, w_all, wo, fnw, on_lanes(a_re), on_lanes(a_im),
                      jnp.swapaxes(bcat, 1, 2), jnp.swapaxes(ccat, 1, 2), on_lanes(s5_d.astype(F32)),
                      jnp.swapaxes(wglu, 1, 2), on_lanes(hlb[:, 0:3, :]), on_lanes(mlb[:, 0:8]))
    return prompt_weights, sample_weights


def _trunk(x, pos, states, weights, *, NB, T, nvalid):
    s5re, s5im, hg, mc, mn, mm, rt = states
    d, b = s5re.shape[0], s5re.shape[1]
    kstates = (s5re.reshape(d, b, S5_N), s5im.reshape(d, b, S5_N), hg, mc, mn.reshape(d, b, 256), mm, rt)
    cos_tab, sin_tab = _rotary_tables(pos)
    out = None
    for l in range(DEPTH):
        x, out = _run_layer(l, x, cos_tab, sin_tab, kstates, out, weights, NB=NB, T=T, nvalid=nvalid)
    o_s5re, o_s5im, o_hg, o_mc, o_mn, o_mm, o_rt = out
    return x, [
        o_s5re.reshape(d, b, S5_GROUPS, S5_STATE), o_s5im.reshape(d, b, S5_GROUPS, S5_STATE),
        o_hg, o_mc, o_mn.reshape(d, b, N_HEADS, HEAD_W), o_mm, o_rt,
    ]


def _prompt_trunk(x, states, weights):
    b, l, _ = x.shape
    return _trunk(x, jnp.arange(l, dtype=jnp.int32), states, weights, NB=b, T=PROMPT_T, nvalid=PROMPT_T)


def _sample_trunk(x, states, weights):
    s5re, s5im, hg, mc, mn, mm, rt = states
    d, b, ls = s5re.shape[0], s5re.shape[1], x.shape[1]
    kstates = (
        jnp.transpose(s5re, (0, 2, 3, 1)).reshape(d, S5_N, b), jnp.transpose(s5im, (0, 2, 3, 1)).reshape(d, S5_N, b),
        jnp.transpose(hg, (0, 2, 3, 4, 1)), jnp.transpose(mc, (0, 2, 3, 4, 1)),
        jnp.transpose(mn, (0, 2, 3, 1)).reshape(d, N_HEADS * HEAD_W, b), jnp.transpose(mm, (0, 2, 1)),
        jnp.transpose(rt, (0, 2, 3, 4, 1)),
    )
    half = RET_DK // 2
    inv_freq = ROPE_BASE ** (-jnp.arange(half, dtype=F32) / half)
    ang = (PAST_LEN + jnp.arange(ls, dtype=jnp.int32)).astype(F32)[:, None] * inv_freq[None, :]
    tables = (jnp.broadcast_to(jnp.cos(ang)[:, :, None], (ls, half, b)),
              jnp.broadcast_to(jnp.sin(ang)[:, :, None], (ls, half, b)))
    xt = jnp.transpose(x, (1, 0, 2))
    out = None
    for l in range(DEPTH):
        xt, out = _run_sample_layer(l, xt, kstates, out, weights + tables)
    o_s5re, o_s5im, o_hg, o_mc, o_mn, o_mm, o_rt = out
    return jnp.transpose(xt, (1, 0, 2)), [
        jnp.transpose(o_s5re.reshape(d, S5_GROUPS, S5_STATE, b), (0, 3, 1, 2)),
        jnp.transpose(o_s5im.reshape(d, S5_GROUPS, S5_STATE, b), (0, 3, 1, 2)),
        jnp.transpose(o_hg, (0, 4, 1, 2, 3)), jnp.transpose(o_mc, (0, 4, 1, 2, 3)),
        jnp.transpose(o_mn.reshape(d, N_HEADS, HEAD_W, b), (0, 3, 1, 2)), jnp.transpose(o_mm, (0, 2, 1)),
        jnp.transpose(o_rt, (0, 4, 1, 2, 3)),
    ]


def kernel(x_prompt, x_sample, state_s5_re, state_s5_im, state_hgrn, state_mlstm_C, state_mlstm_n,
           state_mlstm_m, state_ret, norm_w, w_in, w_out, final_norm_w, s5_lambda_re, s5_lambda_im,
           s5_B_re, s5_B_im, s5_C_re, s5_C_im, s5_D, s5_log_dt, s5_w_glu, hgrn_lb_logits,
           mlstm_b_i, mlstm_b_f):
    prompt_weights, sample_weights = _prepare_weights(
        norm_w, w_in, w_out, final_norm_w, s5_lambda_re, s5_lambda_im, s5_B_re, s5_B_im,
        s5_C_re, s5_C_im, s5_D, s5_log_dt, s5_w_glu, hgrn_lb_logits, mlstm_b_i, mlstm_b_f)
    bp = x_prompt.shape[0]
    zero_states = (
        jnp.zeros((DEPTH, bp, S5_GROUPS, S5_STATE), F32), jnp.zeros((DEPTH, bp, S5_GROUPS, S5_STATE), F32),
        jnp.zeros((DEPTH, bp, N_HEADS, HEAD_W, HEAD_W), F32), jnp.zeros((DEPTH, bp, N_HEADS, HEAD_W, HEAD_W), F32),
        jnp.zeros((DEPTH, bp, N_HEADS, HEAD_W), F32), jnp.zeros((DEPTH, bp, N_HEADS), F32),
        jnp.zeros((DEPTH, bp, N_HEADS, RET_DK, HEAD_W), F32),
    )
    y_prompt, p_states = _prompt_trunk(x_prompt, zero_states, prompt_weights)
    sample_states = (state_s5_re, state_s5_im, state_hgrn, state_mlstm_C, state_mlstm_n, state_mlstm_m, state_ret)
    y_sample, d_states = _sample_trunk(x_sample, sample_states, sample_weights)
    return (y_prompt, y_sample, *p_states, *d_states)
```

```python
import functools
import math

import jax
import jax.numpy as jnp
from jax import lax
from jax.experimental import pallas as pl
from jax.experimental.pallas import tpu as pltpu

F32 = jnp.float32
BF16 = jnp.bfloat16

D_MODEL = 1024
DEPTH = 4
BRANCH_W = 256
N_HEADS = 4
HEAD_W = 64
S5_GROUPS = 16
S5_CH = 16
S5_STATE = 64
S5_N = S5_GROUPS * S5_STATE
S5_TILES = S5_N // 128
RET_DK = 32
RET_KW = N_HEADS * RET_DK
ROPE_BASE = 10000.0
EPS = 1e-6
PAST_LEN = 16384

CHUNK_ROWS = 128
STACK = N_HEADS * CHUNK_ROWS
PROMPT_T = 64
VMEM_LIMIT_BYTES = 60 * 1024 * 1024

C_S5 = 0
C_HG = 512
C_ML = 1536
C_RT = 2816
C_GATE = 3584
W_ALL_COLS = 3712
HGRN_SAFE_DECAY = 80.0


def _dot(a, b):
    return jnp.dot(a, b, preferred_element_type=F32)


def _dot_nt(a, b):
    return lax.dot_general(a, b, (((1,), (1,)), ((), ())), preferred_element_type=F32)


def _dot_tn(a, b):
    return lax.dot_general(a, b, (((0,), (0,)), ((), ())), preferred_element_type=F32)


def _split3(x):
    hi = x.astype(BF16)
    r = x - hi.astype(F32)
    mid = r.astype(BF16)
    lo = (r - mid.astype(F32)).astype(BF16)
    return hi, mid, lo


def _sel_dot(sel_bf, x):
    hi, mid, lo = _split3(x)
    return _dot(sel_bf, hi) + _dot(sel_bf, mid) + _dot(sel_bf, lo)


def _sel_dot2(sel_bf, x):
    hi = x.astype(BF16)
    lo = (x - hi.astype(F32)).astype(BF16)
    return _dot(sel_bf, hi) + _dot(sel_bf, lo)


def _dot_sel(x, sel_bf):
    hi = x.astype(BF16)
    lo = (x - hi.astype(F32)).astype(BF16)
    return _dot(hi, sel_bf) + _dot(lo, sel_bf)


def _sigmoid(x):
    return 0.5 + 0.5 * jnp.tanh(0.5 * x)


def _silu(x):
    return x * _sigmoid(x)


def _softplus_neg_abs(x):
    return jnp.log(1.0 + jnp.exp(-jnp.abs(x)))


def _log_sigmoid(x):
    return jnp.minimum(x, 0.0) - _softplus_neg_abs(x)


def _gelu_tanh(x):
    return 0.5 * x * (1.0 + jnp.tanh(math.sqrt(2.0 / math.pi) * (x + 0.044715 * (x * x * x))))


def _iota(shape, dim):
    return lax.broadcasted_iota(jnp.int32, shape, dim)


def _log2(n):
    k = int(round(math.log2(n)))
    assert (1 << k) == n, n
    return k


def _stack_heads(x_bf, head_mask):
    return jnp.where(head_mask, jnp.concatenate([x_bf] * N_HEADS, axis=0), jnp.zeros((), x_bf.dtype))


def _tile_state(s, mask):
    return jnp.where(mask, jnp.concatenate([s] * N_HEADS, axis=1), 0.0)


def _untile_state(s_bd):
    return (s_bd[:, 0:64] + s_bd[:, 64:128]) + (s_bd[:, 128:192] + s_bd[:, 192:256])


def _lanes_per_head(cols):
    return jnp.concatenate([jnp.broadcast_to(c, (CHUNK_ROWS, HEAD_W)) for c in cols], axis=1)


def _ret_log_decay(head):
    out = jnp.full(head.shape, math.log(1.0 - 2.0 ** -5.0), F32)
    for hh in range(1, N_HEADS):
        out = jnp.where(head == hh, math.log(1.0 - 2.0 ** (-5.0 - hh)), out)
    return out


N_STATES = 7
N_INPUTS = 23
N_SAMPLE_INPUTS = 22
SAMPLE_LANES = 128


def _layer_kernel(*refs, NB, T, final_norm, n_alias):
    (x_ref, xn_ref, cos_ref, sin_ref, s5re_in, s5im_in, hg_in, mc_in, mn_in, mm_in, rt_in,
     nw_ref, w_ref, wo_ref, fnw_ref, are_ref, aim_ref, bcat_ref, ccat_ref, dsk_ref,
     wglu_ref, hlb_ref, mlb_ref) = refs[:N_INPUTS]
    (y_ref, s5re_out, s5im_out, hg_out, mc_out, mn_out, mm_out, rt_out,
     pj_s, pjn_s, hn_s, bu_s, mix_s, hre_s, him_s, sthg_s, stmc_s, strt_s, stn_s, stm_s,
     perm_s, permt_s, sc_s) = refs[N_INPUTS + n_alias:]
    R = NB * T
    NCH = R // CHUNK_ROWS
    nseg = CHUNK_ROWS // T
    last = T - 1
    lgT = _log2(T)
    lgNB = _log2(NB)
    CR = CHUNK_ROWS
    j = pl.program_id(1)
    nj = pl.num_programs(1)

    bd_mask = (_iota((256, 256), 0) >> 6) == (_iota((256, 256), 1) >> 6)
    rt_mask = (_iota((RET_KW, 256), 0) >> 5) == (_iota((RET_KW, 256), 1) >> 6)
    hm256 = (_iota((STACK, 256), 0) >> 7) == (_iota((STACK, 256), 1) >> 6)
    hm128 = (_iota((STACK, RET_KW), 0) >> 7) == (_iota((STACK, RET_KW), 1) >> 5)
    bd_ones = bd_mask.astype(BF16)
    head_expand = hm256.astype(BF16)

    t_st = _iota((CR, STACK), 0)
    s_st = _iota((CR, STACK), 1) & (CR - 1)
    causal_st = ((t_st >> lgT) == (s_st >> lgT)) & (t_st >= s_st)
    diag_st = t_st == s_st
    t_sq = _iota((CR, CR), 0)
    s_sq = _iota((CR, CR), 1)
    causal_sq = ((t_sq >> lgT) == (s_sq >> lgT)) & (t_sq >= s_sq)
    triseg = causal_sq.astype(BF16)
    tl128 = _iota((CR, 128), 0) & (T - 1)
    row256 = _iota((CR, 256), 0)

    @pl.when(j == 0)
    def _():
        ri = _iota((R, R), 0)
        ci = _iota((R, R), 1)
        perm_s[...] = (ci == (((ri & (NB - 1)) << lgT) + (ri >> lgNB))).astype(BF16)
        permt_s[...] = (ri == (((ci & (NB - 1)) << lgT) + (ci >> lgNB))).astype(BF16)
        hre_s[...] = s5re_in[...]
        him_s[...] = s5im_in[...]
        stn_s[...] = mn_in[...]
        m_in = mm_in[...]
        stm_s[...] = jnp.concatenate(
            [jnp.broadcast_to(m_in[:, hh:hh + 1], (NB, HEAD_W)) for hh in range(N_HEADS)], axis=1)
        for b in range(NB):
            sthg_s[b] = jnp.transpose(_tile_state(hg_in[b].reshape(256, HEAD_W), bd_mask))
            stmc_s[b] = _tile_state(mc_in[b].reshape(256, HEAD_W), bd_mask)
            strt_s[b] = _tile_state(rt_in[b].reshape(RET_KW, HEAD_W), rt_mask)

    def project(src_ref, dst_ref):
        xs = src_ref[...].reshape(R, D_MODEL)
        ms = jnp.mean(xs * xs, axis=-1, keepdims=True)
        hn = (xs * lax.rsqrt(ms + EPS) * nw_ref[...]).astype(BF16)
        for i in range(C_GATE // 256):
            dst_ref[:, 256 * i:256 * (i + 1)] = _dot_nt(hn, w_ref[256 * i:256 * (i + 1), :])
        dst_ref[:, C_GATE:W_ALL_COLS] = _dot_nt(hn, w_ref[C_GATE:W_ALL_COLS, :])

    @pl.when(j == 0)
    def _():
        project(x_ref, pj_s)

    xs = xn_ref[...].reshape(R, D_MODEL)
    ms_n = jnp.mean(xs * xs, axis=-1, keepdims=True)
    hn_s[...] = (xs * lax.rsqrt(ms_n + EPS) * nw_ref[...]).astype(BF16)

    def proj_tile(c0, c1):
        def task():
            pjn_s[:, c0:c1] = _dot_nt(hn_s[...], w_ref[c0:c1, :])
        return task

    fillers = [proj_tile(256 * i, 256 * (i + 1)) for i in range(C_GATE // 256)]
    fillers.append(proj_tile(C_GATE, W_ALL_COLS))

    def fill(n=1):
        for _ in range(min(n, len(fillers))):
            fillers.pop(0)()

    def head_norm(o):
        mu = _dot_sel(o, bd_ones) * (1.0 / HEAD_W)
        d = o - mu
        var = _dot((d * d).astype(BF16), bd_ones) * (1.0 / HEAD_W)
        return d * lax.rsqrt(var + EPS)

    def for_chunks(fn):
        for ch in range(NCH):
            fill()
            fn(ch)

    def seg_last_rows(a):
        w = a.shape[1]
        pieces = [jnp.broadcast_to(a[sg * T + last:sg * T + last + 1, :], (T, w)) for sg in range(nseg)]
        return pieces[0] if nseg == 1 else jnp.concatenate(pieces, axis=0)

    def chunk_rows(ch):
        if isinstance(ch, int):
            return pl.ds(ch * CR, CR)
        return pl.ds(pl.multiple_of(ch * CR, CR), CR)

    a_re = are_ref[...]
    a_im = aim_ref[...]

    def s5_drive():
        u = pj_s[:, C_S5:C_S5 + 256]
        ub = _dot(perm_s[...], u.astype(BF16)).astype(BF16)
        for i in range(S5_TILES):
            drive = _dot(ub, bcat_ref[:, 256 * i:256 * (i + 1)])
            bu_s[2 * i] = drive[:, 0:128]
            bu_s[2 * i + 1] = drive[:, 128:256]

    def s5_step(t, carry):
        idx = pl.ds(t * NB, NB)
        new = []
        for c in range(S5_TILES):
            hre, him = carry[2 * c], carry[2 * c + 1]
            ar = a_re[:, 128 * c:128 * (c + 1)]
            ai = a_im[:, 128 * c:128 * (c + 1)]
            nre = ar * hre - ai * him + bu_s[c, idx, :]
            nim = ar * him + ai * hre + bu_s[S5_TILES + c, idx, :]
            bu_s[c, idx, :] = nre
            bu_s[S5_TILES + c, idx, :] = nim
            new += [nre, nim]
        return tuple(new)

    def s5_scan():
        fin = []
        for c in range(S5_TILES):
            fin += [hre_s[:, 128 * c:128 * (c + 1)], him_s[:, 128 * c:128 * (c + 1)]]
        fin = tuple(fin)
        for t in range(T):
            fin = s5_step(t, fin)
        for c in range(S5_TILES):
            hre_s[:, 128 * c:128 * (c + 1)] = fin[2 * c]
            him_s[:, 128 * c:128 * (c + 1)] = fin[2 * c + 1]

    def s5_output():
        u = pj_s[:, C_S5:C_S5 + 256]
        hid = jnp.concatenate([bu_s[c].astype(BF16) for c in range(2 * S5_TILES)], axis=1)
        y = _sel_dot2(permt_s[...], _dot(hid, ccat_ref[...])) + dsk_ref[...] * u
        fill()
        y = _gelu_tanh(y)
        y = y * _sigmoid(_dot(y.astype(BF16), wglu_ref[...]))
        mix_s[:, 0:256] = (y * _silu(pj_s[:, C_S5 + 256:C_S5 + 512])).astype(BF16)

    log_lb = hlb_ref[0:1, :]
    log1m_lb = hlb_ref[1:2, :]
    one_m_lb = hlb_ref[2:3, :]

    def boundary_rows(c, m):
        blk = 2 * m
        if blk >= 8:
            c3 = c.reshape(CR // blk, blk, 256)
            return jnp.broadcast_to(c3[:, m - 1:m, :], (CR // blk, blk, 256)).reshape(CR, 256)
        c3 = c.reshape(CR // 8, 8, 256)
        sub = _iota((CR // 8, 8, 256), 1)
        out = None
        for jb in range(8 // blk):
            piece = jnp.broadcast_to(c3[:, jb * blk + m - 1:jb * blk + m, :], (CR // 8, 8, 256))
            out = piece if out is None else jnp.where((sub >> _log2(blk)) == jb, piece, out)
        return out.reshape(CR, 256)

    def hgrn_gates(rows):
        q = _silu(pj_s[rows, C_HG:C_HG + 256])
        xf = pj_s[rows, C_HG + 256:C_HG + 512]
        b_ = log1m_lb + _log_sigmoid(xf)
        log_f = jnp.maximum(log_lb, b_) + _softplus_neg_abs(log_lb - b_)
        nlf = -log_f
        hk = one_m_lb * _sigmoid(-xf)
        return q, hk, _sel_dot(triseg, nlf)

    def hgrn_scores_any_decay(q, hk, c):
        scores = jnp.where(diag_st, _dot_nt(q.astype(BF16), _stack_heads(hk.astype(BF16), hm256)), 0.0)
        m = T // 2
        while m >= 1:
            e = jnp.exp(-jnp.abs(c - boundary_rows(c, m)))
            second = (row256 & (2 * m - 1)) >= m
            qe = jnp.where(second, q * e, 0.0).astype(BF16)
            ke = jnp.where(second, 0.0, hk * e).astype(BF16)
            lg = _log2(2 * m)
            same_blk = (t_st >> lg) == (s_st >> lg)
            scores = scores + jnp.where(same_blk, _dot_nt(qe, _stack_heads(ke, hm256)), 0.0)
            m //= 2
        return scores

    s5_drive()
    fill()
    s5_scan()
    s5_output()
    fill()
    gates = [hgrn_gates(chunk_rows(ch)) for ch in range(NCH)]
    c_max = gates[0][2]
    for ch in range(1, NCH):
        c_max = jnp.maximum(c_max, gates[ch][2])
    c_max = jnp.max(c_max)
    for ch in range(NCH):
        q, hk, c = gates[ch]
        ke = _stack_heads((hk * jnp.exp(c)).astype(BF16), hm256)
        sc_s[ch] = jnp.where(causal_st, _dot_nt((q * jnp.exp(-c)).astype(BF16), ke), 0.0).astype(BF16)

    @pl.when(c_max > HGRN_SAFE_DECAY)
    def _():
        def redo(ch, carry):
            q, hk, c = hgrn_gates(chunk_rows(ch))
            sc_s[ch] = hgrn_scores_any_decay(q, hk, c).astype(BF16)
            return carry
        lax.fori_loop(0, NCH, redo, 0)

    def hgrn_chunk(ch):
        rows = chunk_rows(ch)
        q, hk, c = gates[ch]
        v = pj_s[rows, C_HG + 512:C_HG + 768].astype(BF16)
        gate = pj_s[rows, C_HG + 768:C_HG + 1024]
        o = _dot(sc_s[ch], _stack_heads(v, hm256))

        qd = (q * jnp.exp(-c)).astype(BF16)
        kd = (hk * jnp.exp(-(seg_last_rows(c) - c))).astype(BF16)
        o_inter = []
        for sg in range(nseg):
            b = ch * nseg + sg
            sl = slice(sg * T, (sg + 1) * T)
            st = sthg_s[b]
            o_inter.append(_dot_nt(qd[sl], st.astype(BF16)))
            c_last = c[sg * T + last:sg * T + last + 1, :]
            upd = _dot_tn(v[sl], kd[sl])
            sthg_s[b] = st * jnp.exp(-c_last) + jnp.where(bd_mask, upd, 0.0)
        o = o + (o_inter[0] if nseg == 1 else jnp.concatenate(o_inter, axis=0))
        mix_s[rows, 256:512] = (head_norm(o) * _silu(gate)).astype(BF16)

    for_chunks(hgrn_chunk)

    lane128 = _iota((CR, 128), 1)
    is_i = lane128 < N_HEADS
    is_f = (lane128 >= N_HEADS) & (lane128 < 2 * N_HEADS)

    def mlstm_chunk(ch):
        rows = chunk_rows(ch)
        q = pj_s[rows, C_ML:C_ML + 256]
        k = pj_s[rows, C_ML + 256:C_ML + 512] * (HEAD_W ** -0.5)
        v = pj_s[rows, C_ML + 512:C_ML + 768].astype(BF16)
        og = pj_s[rows, C_ML + 768:C_ML + 1024]
        gate = pj_s[rows, C_ML + 1024:C_ML + 1280]
        gpre = pj_s[rows, C_GATE:W_ALL_COLS] + mlb_ref[...]
        log_i = gpre
        log_fg = _log_sigmoid(gpre)
        cum =_sel_dot(triseg, jnp.where(is_f, log_fg, 0.0))
        g = jnp.where(is_i, log_i, cum)
        g_t = jnp.transpose(g)

        if nseg == 1:
            ms_rows = jnp.broadcast_to(stm_s[pl.ds(ch, 1), :], (CR, 256))
        else:
            ms_rows = jnp.concatenate(
                [jnp.broadcast_to(stm_s[pl.ds(ch * nseg + sg, 1), :], (T, 256)) for sg in range(nseg)], axis=0)

        d_blocks, mt_cols, wp_cols, cf_cols, ig_cols = [], [], [], [], []
        for hh in range(N_HEADS):
            cf_col = g[:, N_HEADS + hh:N_HEADS + hh + 1]
            ig_col = g[:, hh:hh + 1]
            row_f = jnp.broadcast_to(g_t[N_HEADS + hh:N_HEADS + hh + 1, :], (CR, CR))
            row_i = jnp.broadcast_to(g_t[hh:hh + 1, :], (CR, CR))
            log_d = jnp.where(causal_sq, jnp.broadcast_to(cf_col, (CR, CR)) - row_f + row_i, -jnp.inf)
            w_prev = cf_col + ms_rows[:, hh * HEAD_W:hh * HEAD_W + 1]
            m_t = jnp.maximum(w_prev, jnp.max(log_d, axis=1, keepdims=True))
            d_blocks.append(jnp.exp(log_d - m_t))
            mt_cols.append(m_t)
            wp_cols.append(w_prev)
            cf_cols.append(cf_col)
            ig_cols.append(ig_col)
        d_mat = jnp.concatenate(d_blocks, axis=1)
        mt = _lanes_per_head(mt_cols)
        s_prev = jnp.exp(_lanes_per_head(wp_cols) - mt)
        cumf = _lanes_per_head(cf_cols)
        lig = _lanes_per_head(ig_cols)

        qb = q.astype(BF16)
        scores = _dot_nt(qb, _stack_heads(k.astype(BF16), hm256)) * d_mat
        num = _dot(scores.astype(BF16), _stack_heads(v, hm256))
        den = _dot_sel(scores, head_expand)

        w_k = jnp.exp(seg_last_rows(cumf) - cumf + lig - seg_last_rows(mt))
        kw = w_k * k
        kwb = kw.astype(BF16)
        qc, qn = [], []
        for sg in range(nseg):
            b = ch * nseg + sg
            sl = slice(sg * T, (sg + 1) * T)
            cs = stmc_s[b]
            ns = stn_s[pl.ds(b, 1), :]
            ms_b = stm_s[pl.ds(b, 1), :]
            qc.append(_dot(qb[sl], cs.astype(BF16)))
            qn.append(_dot_sel(q[sl] * ns, bd_ones))
            lr = sg * T + last
            m_new = mt[lr:lr + 1, :]
            carry = jnp.exp(cumf[lr:lr + 1, :] + ms_b - m_new)
            stmc_s[b] = carry * cs + jnp.where(bd_mask, _dot_tn(kwb[sl], v[sl]), 0.0)
            stn_s[pl.ds(b, 1), :] = carry * ns + jnp.sum(kw[sl], axis=0, keepdims=True)
            stm_s[pl.ds(b, 1), :] = m_new
        qc = qc[0] if nseg == 1 else jnp.concatenate(qc, axis=0)
        qn = qn[0] if nseg == 1 else jnp.concatenate(qn, axis=0)
        num = s_prev * qc + num
        den = s_prev * qn + den
        hcell = num / jnp.maximum(jnp.abs(den), jnp.exp(-mt))
        mix_s[rows, 512:768] = (head_norm(_sigmoid(og) * hcell) * _silu(gate)).astype(BF16)

    for_chunks(mlstm_chunk)

    gam_st = _ret_log_decay(_iota((CR, STACK), 1) >> 7)
    d_ret = jnp.where(causal_st, jnp.exp(((t_st - s_st) & (T - 1)).astype(F32) * gam_st), 0.0)
    gam128 = _ret_log_decay(_iota((CR, 128), 1) >> 5)
    gam256 = _ret_log_decay(_iota((1, 256), 1) >> 6)
    q_dec = jnp.exp((tl128 + 1).astype(F32) * gam128)
    k_dec = jnp.exp((last - tl128).astype(F32) * gam128)
    s_dec = jnp.exp(float(last + 1) * gam256)
    first_half = (_iota((CR, 128), 1) & (RET_DK - 1)) < (RET_DK // 2)

    def rotary(t, cos, sin_signed):
        partner = jnp.where(first_half, pltpu.roll(t, 128 - RET_DK // 2, axis=1), pltpu.roll(t, RET_DK // 2, axis=1))
        return t * cos + partner * sin_signed

    def ret_chunk(ch):
        rows = chunk_rows(ch)
        cos = jnp.concatenate([cos_ref[...]] * nseg, axis=0)
        sin_signed = jnp.concatenate([sin_ref[...]] * nseg, axis=0)
        q = rotary(pj_s[rows, C_RT:C_RT + 128], cos, sin_signed)
        k = rotary(pj_s[rows, C_RT + 128:C_RT + 256], cos, sin_signed) * (RET_DK ** -0.5)
        v = pj_s[rows, C_RT + 256:C_RT + 512].astype(BF16)
        gate = pj_s[rows, C_RT + 512:C_RT + 768]
        scores = _dot_nt(q.astype(BF16), _stack_heads(k.astype(BF16), hm128)) * d_ret
        o = _dot(scores.astype(BF16), _stack_heads(v, hm256))
        qd = (q * q_dec).astype(BF16)
        kd = (k * k_dec).astype(BF16)
        o_inter = []
        for sg in range(nseg):
            b = ch * nseg + sg
            sl = slice(sg * T, (sg + 1) * T)
            st = strt_s[b]
            o_inter.append(_dot(qd[sl], st.astype(BF16)))
            strt_s[b] = st * s_dec + jnp.where(rt_mask, _dot_tn(kd[sl], v[sl]), 0.0)
        o = o + (o_inter[0] if nseg == 1 else jnp.concatenate(o_inter, axis=0))
        mix_s[rows, 768:1024] = (head_norm(o) * _silu(gate)).astype(BF16)

    for_chunks(ret_chunk)

    fill(len(fillers))
    out = x_ref[...].reshape(R, D_MODEL) + _dot(mix_s[...], wo_ref[...])
    if final_norm:
        ms2 = jnp.mean(out * out, axis=-1, keepdims=True)
        out = out * lax.rsqrt(ms2 + EPS) * fnw_ref[...]
    y_ref[...] = out.reshape(NB, T, D_MODEL)

    @pl.when(j == nj - 1)
    def _():
        s5re_out[...] = hre_s[...]
        s5im_out[...] = him_s[...]
        mn_out[...] = stn_s[...]
        mm_out[...] = jnp.concatenate([stm_s[:, hh * HEAD_W:hh * HEAD_W + 1] for hh in range(N_HEADS)], axis=1)
        for b in range(NB):
            hg_out[b] = _untile_state(jnp.transpose(sthg_s[b])).reshape(N_HEADS, HEAD_W, HEAD_W)
            mc_out[b] = _untile_state(stmc_s[b]).reshape(N_HEADS, HEAD_W, HEAD_W)
            rt_out[b] = _untile_state(strt_s[b]).reshape(N_HEADS, RET_DK, HEAD_W)

    pj_s[...] = pjn_s[...]


def _run_layer(l, x, cos_tab, sin_tab, states, prev_out, weights, *, NB, T):
    B, L, _ = x.shape
    nbb, ntc = B // NB, L // T
    R = NB * T

    def layer_spec(shape, first, single_buffer=False):
        rest = tuple(shape[2:])
        zeros = (0,) * len(rest)
        kw = {"pipeline_mode": pl.Buffered(1)} if single_buffer else {}
        if first is None:
            return pl.BlockSpec((None, shape[1]) + rest, lambda i, j: (l, 0) + zeros, **kw)
        return pl.BlockSpec((None, first) + rest, lambda i, j: (l, i) + zeros, **kw)

    state_specs = [layer_spec(s.shape, NB) for s in states]
    in_specs = [
        pl.BlockSpec((NB, T, D_MODEL), lambda i, j: (i, j, 0)),
        pl.BlockSpec((NB, T, D_MODEL), lambda i, j: (i, jnp.minimum(j + 1, ntc - 1), 0)),
        pl.BlockSpec((T, 128), lambda i, j: (j, 0)),
        pl.BlockSpec((T, 128), lambda i, j: (j, 0)),
    ] + state_specs + [layer_spec(w.shape, None, single_buffer=True) for w in weights]
    assert len(in_specs) == N_INPUTS
    operands = [x, x, cos_tab, sin_tab, *states, *weights]
    aliases = {}
    if prev_out is not None:
        in_specs += [pl.BlockSpec(memory_space=pl.ANY)] * N_STATES
        aliases = {N_INPUTS + k: 1 + k for k in range(N_STATES)}
        operands += list(prev_out)
    out_shape = [jax.ShapeDtypeStruct(x.shape, F32)] + [jax.ShapeDtypeStruct(s.shape, F32) for s in states]
    out_specs = [pl.BlockSpec((NB, T, D_MODEL), lambda i, j: (i, j, 0))] + state_specs
    scratch = [
        pltpu.VMEM((R, W_ALL_COLS), F32),
        pltpu.VMEM((R, W_ALL_COLS), F32),
        pltpu.VMEM((R, D_MODEL), BF16),
        pltpu.VMEM((2 * S5_TILES, R, 128), F32),
        pltpu.VMEM((R, D_MODEL), BF16),
        pltpu.VMEM((NB, S5_N), F32),
        pltpu.VMEM((NB, S5_N), F32),
        pltpu.VMEM((NB, 256, 256), F32),
        pltpu.VMEM((NB, 256, 256), F32),
        pltpu.VMEM((NB, RET_KW, 256), F32),
        pltpu.VMEM((NB, 256), F32),
        pltpu.VMEM((NB, 256), F32),
        pltpu.VMEM((R, R), BF16),
        pltpu.VMEM((R, R), BF16),
        pltpu.VMEM((R // CHUNK_ROWS, CHUNK_ROWS, STACK), BF16),
    ]
    kern = functools.partial(_layer_kernel, NB=NB, T=T, final_norm=(l == DEPTH - 1), n_alias=len(aliases))
    outs = pl.pallas_call(
        kern,
        grid=(nbb, ntc),
        in_specs=in_specs,
        out_specs=out_specs,
        out_shape=out_shape,
        scratch_shapes=scratch,
        input_output_aliases=aliases,
        compiler_params=pltpu.CompilerParams(
            dimension_semantics=("arbitrary", "arbitrary"),
            vmem_limit_bytes=VMEM_LIMIT_BYTES,
        ),
    )(*operands)
    return outs[0], outs[1:]


def _sample_kernel(*refs, L, final_norm, n_alias):
    (x_ref, s5re_in, s5im_in, hg_in, mc_in, mn_in, mm_in, rt_in,
     nw_ref, wt_ref, wo_ref, fnw_ref, are_ref, aim_ref, bcatt_ref, ccatt_ref, dsk_ref, wglut_ref,
     hlb_ref, mlb_ref, cos_ref, sin_ref) = refs[:N_SAMPLE_INPUTS]
    (y_ref, s5re_out, s5im_out, hg_out, mc_out, mn_out, mm_out, rt_out,
     pj_s, bu_s, hid_s, mix_s, a_s, b_s, c_s, d_s) = refs[N_SAMPLE_INPUTS + n_alias:]
    h = pl.program_id(0)
    nh = pl.num_programs(0)
    B = SAMPLE_LANES
    half = RET_DK // 2

    def cols(t):
        return slice(t * B, (t + 1) * B)

    def tile_tokens(p):
        return jnp.concatenate([p] * L, axis=1)

    @pl.when(h == 0)
    def _():
        hs = []
        for t in range(L):
            xt = x_ref[t]
            ms = jnp.mean(xt * xt, axis=-1, keepdims=True)
            hs.append((xt * lax.rsqrt(ms + EPS) * nw_ref[...]).astype(BF16))
        hb = jnp.concatenate(hs, axis=0)
        for r0 in range(0, W_ALL_COLS, 256):
            r1 = min(r0 + 256, W_ALL_COLS)
            pj_s[r0:r1, :] = _dot_nt(wt_ref[r0:r1, :], hb)

        u = pj_s[C_S5:C_S5 + 256, :]
        bu_s[...] = _dot(bcatt_ref[...], u.astype(BF16))
        hre, him = s5re_in[...], s5im_in[...]
        a_re, a_im = are_ref[...], aim_ref[...]
        for t in range(L):
            nre = a_re * hre - a_im * him + bu_s[0:S5_N, cols(t)]
            nim = a_re * him + a_im * hre + bu_s[S5_N:2 * S5_N, cols(t)]
            hid_s[0:S5_N, cols(t)] = nre.astype(BF16)
            hid_s[S5_N:2 * S5_N, cols(t)] = nim.astype(BF16)
            hre, him = nre, nim
        s5re_out[...] = hre
        s5im_out[...] = him
        y = _dot(ccatt_ref[...], hid_s[...]) + tile_tokens(dsk_ref[...]) * u
        y = _gelu_tanh(y)
        y = y * _sigmoid(_dot(wglut_ref[...], y.astype(BF16)))
        mix_s[0:256, :] = (y * _silu(pj_s[C_S5 + 256:C_S5 + 512, :])).astype(BF16)

    def head_rows(base, width=HEAD_W):
        return pl.ds(pl.multiple_of(base + h * width, width), width)

    def head_norm_rows(o):
        mu = jnp.mean(o, axis=0, keepdims=True)
        d = o - mu
        return d * lax.rsqrt(jnp.mean(d * d, axis=0, keepdims=True) + EPS)

    def row_on_sublanes(ref, t, k):
        return jnp.broadcast_to(ref[t, pl.ds(k, 1), :], (HEAD_W, B))

    def recur(nk, st_in, st_out, decay, key_s, query_s, value_s):
        def body(k, outs):
            s = st_in[k]
            new = []
            for t in range(L):
                s = decay(t, k) * s + row_on_sublanes(key_s, t, k) * value_s[t]
                new.append(outs[t] + row_on_sublanes(query_s, t, k) * s)
            st_out[k] = s
            return tuple(new)
        return lax.fori_loop(0, nk, body, (jnp.zeros((HEAD_W, B), F32),) * L, unroll=2)

    hrows = pl.ds(pl.multiple_of(h * HEAD_W, HEAD_W), HEAD_W)
    log_lb, log1m_lb, one_m_lb = hlb_ref[0, hrows, :], hlb_ref[1, hrows, :], hlb_ref[2, hrows, :]
    for t in range(L):
        xf = pj_s[head_rows(C_HG + 256), cols(t)]
        b_ = log1m_lb + _log_sigmoid(xf)
        log_f = jnp.maximum(log_lb, b_) + _softplus_neg_abs(log_lb - b_)
        a_s[t] = jnp.exp(log_f)
        b_s[t] = one_m_lb * _sigmoid(-xf)
        c_s[t] = _silu(pj_s[head_rows(C_HG), cols(t)])
        d_s[t] = pj_s[head_rows(C_HG + 512), cols(t)]
    outs = recur(HEAD_W, hg_in, hg_out, lambda t, k: row_on_sublanes(a_s, t, k), b_s, c_s, d_s)
    for t in range(L):
        gate = pj_s[head_rows(C_HG + 768), cols(t)]
        mix_s[head_rows(256), cols(t)] = (head_norm_rows(outs[t]) * _silu(gate)).astype(BF16)

    m_prev = mm_in[pl.ds(h, 1), :]
    n_vec = mn_in[...]
    gate_row = _iota((2 * N_HEADS, B), 0)
    dens, inv_caps = [], []
    for t in range(L):
        gates = pj_s[C_GATE:C_GATE + 2 * N_HEADS, cols(t)] + mlb_ref[...]
        log_i = jnp.sum(jnp.where(gate_row == h, gates, 0.0), axis=0, keepdims=True)
        log_fg = _log_sigmoid(jnp.sum(jnp.where(gate_row == h + N_HEADS, gates, 0.0), axis=0, keepdims=True))
        m_t = jnp.maximum(log_fg + m_prev, log_i)
        f_dec = jnp.exp(log_fg + m_prev - m_t)
        q = pj_s[head_rows(C_ML), cols(t)]
        kw = jnp.exp(log_i - m_t) * (pj_s[head_rows(C_ML + 256), cols(t)] * (HEAD_W ** -0.5))
        n_vec = f_dec * n_vec + kw
        dens.append(jnp.sum(q * n_vec, axis=0, keepdims=True))
        inv_caps.append(jnp.exp(-m_t))
        a_s[t] = jnp.broadcast_to(f_dec, (HEAD_W, B))
        b_s[t] = kw
        c_s[t] = q
        d_s[t] = pj_s[head_rows(C_ML + 512), cols(t)]
        m_prev = m_t
    outs = recur(HEAD_W, mc_in, mc_out, lambda t, k: a_s[t], b_s, c_s, d_s)
    mn_out[...] = n_vec
    mm_out[pl.ds(h, 1), :] = m_prev
    for t in range(L):
        cell = outs[t] / jnp.maximum(jnp.abs(dens[t]), inv_caps[t])
        og = pj_s[head_rows(C_ML + 768), cols(t)]
        gate = pj_s[head_rows(C_ML + 1024), cols(t)]
        mix_s[head_rows(512), cols(t)] = (head_norm_rows(_sigmoid(og) * cell) * _silu(gate)).astype(BF16)

    decay = jnp.float32(1.0 - 2.0 ** -5.0)
    for hh in range(1, N_HEADS):
        decay = jnp.where(h == hh, jnp.float32(1.0 - 2.0 ** (-5.0 - hh)), decay)
    for t in range(L):
        cos, sin = cos_ref[t], sin_ref[t]
        for base, dst, scale in ((C_RT, c_s, 1.0), (C_RT + RET_KW, b_s, RET_DK ** -0.5)):
            t1 = pj_s[pl.ds(pl.multiple_of(base + h * RET_DK, half), half), cols(t)] * scale
            t2 = pj_s[pl.ds(pl.multiple_of(base + h * RET_DK + half, half), half), cols(t)] * scale
            dst[t, 0:half, :] = t1 * cos - t2 * sin
            dst[t, half:RET_DK, :] = t1 * sin + t2 * cos
        d_s[t] = pj_s[head_rows(C_RT + 256), cols(t)]
    outs = recur(RET_DK, rt_in, rt_out, lambda t, k: decay, b_s, c_s, d_s)
    for t in range(L):
        gate = pj_s[head_rows(C_RT + 512), cols(t)]
        mix_s[head_rows(768), cols(t)] = (head_norm_rows(outs[t]) * _silu(gate)).astype(BF16)

    @pl.when(h == nh - 1)
    def _():
        yn = _dot_tn(mix_s[...], wo_ref[...])
        for t in range(L):
            out = x_ref[t] + yn[t * B:(t + 1) * B, :]
            if final_norm:
                ms2 = jnp.mean(out * out, axis=-1, keepdims=True)
                out = out * lax.rsqrt(ms2 + EPS) * fnw_ref[...]
            y_ref[t] = out


def _run_sample_layer(l, x, states, prev_out, weights):
    L, B, _ = x.shape
    assert B == SAMPLE_LANES

    def lspec(shape, per_head, single_buffer=False):
        kw = {"pipeline_mode": pl.Buffered(1)} if single_buffer else {}
        if per_head == "axis":
            rest = tuple(shape[2:])
            return pl.BlockSpec((None, None) + rest, lambda h: (l, h) + (0,) * len(rest), **kw)
        if per_head == "rows":
            return pl.BlockSpec((None, HEAD_W, shape[2]), lambda h: (l, h, 0), **kw)
        rest = tuple(shape[1:])
        return pl.BlockSpec((None,) + rest, lambda h: (l,) + (0,) * len(rest), **kw)

    s5re, s5im, hg, mc, mn, mm, rt = states
    state_specs = [lspec(s5re.shape, None), lspec(s5im.shape, None), lspec(hg.shape, "axis"), lspec(mc.shape, "axis"),
                   lspec(mn.shape, "rows"), lspec(mm.shape, None), lspec(rt.shape, "axis")]
    (nw, wt, wo, fnw, are_b, aim_b, bcatt, ccatt, dsk_b, wglut, hlb_b, mlb_b, cos_t, sin_t) = weights
    table_spec = pl.BlockSpec(cos_t.shape, lambda h: (0, 0, 0))
    in_specs = ([pl.BlockSpec(x.shape, lambda h: (0, 0, 0))] + state_specs
                + [lspec(w.shape, None, single_buffer=True) for w in weights[:-2]] + [table_spec, table_spec])
    assert len(in_specs) == N_SAMPLE_INPUTS
    operands = [x, *states, *weights]
    aliases = {}
    if prev_out is not None:
        in_specs += [pl.BlockSpec(memory_space=pl.ANY)] * N_STATES
        aliases = {N_SAMPLE_INPUTS + k: 1 + k for k in range(N_STATES)}
        operands += list(prev_out)
    n_cols = L * B
    scratch = [
        pltpu.VMEM((W_ALL_COLS, n_cols), F32),
        pltpu.VMEM((2 * S5_N, n_cols), F32),
        pltpu.VMEM((2 * S5_N, n_cols), BF16),
        pltpu.VMEM((D_MODEL, n_cols), BF16),
    ] + [pltpu.VMEM((L, HEAD_W, B), F32)] * 4
    kern = functools.partial(_sample_kernel, L=L, final_norm=(l == DEPTH - 1), n_alias=len(aliases))
    outs = pl.pallas_call(
        kern,
        grid=(N_HEADS,),
        in_specs=in_specs,
        out_specs=[pl.BlockSpec(x.shape, lambda h: (0, 0, 0))] + state_specs,
        out_shape=[jax.ShapeDtypeStruct(x.shape, F32)] + [jax.ShapeDtypeStruct(s.shape, F32) for s in states],
        scratch_shapes=scratch,
        input_output_aliases=aliases,
        compiler_params=pltpu.CompilerParams(
            dimension_semantics=("arbitrary",),
            vmem_limit_bytes=VMEM_LIMIT_BYTES,
        ),
    )(*operands)
    return outs[0], outs[1:]


def _rotary_tables(pos):
    half = RET_DK // 2
    inv_freq = ROPE_BASE ** (-jnp.arange(half, dtype=F32) / half)
    ang = pos.astype(F32)[:, None] * inv_freq[None, :]
    cos, sin = jnp.cos(ang), jnp.sin(ang)
    cos_t = jnp.tile(cos, (1, 128 // half))
    sin_t = jnp.tile(jnp.concatenate([-sin, sin], axis=1), (1, 128 // RET_DK))
    return cos_t, sin_t


def _block_diag(blocks):
    d, g, a, b = blocks.shape
    eye = jnp.eye(g, dtype=blocks.dtype)
    return jnp.einsum('dgab,gh->dgahb', blocks, eye).reshape(d, g * a, g * b)


def _prepare_weights(norm_w, w_in, w_out, final_norm_w, lam_re, lam_im, b_re, b_im, c_re, c_im,
                     s5_d, s5_log_dt, s5_w_glu, hgrn_lb_logits, mlstm_b_i, mlstm_b_f):
    depth = w_in.shape[0]
    w_t = jnp.swapaxes(w_in, 1, 2)
    w_all = jnp.concatenate(
        [w_t[:, 0:2560], w_t[:, 2568:3592], w_t[:, 2560:2568],
         jnp.zeros((depth, W_ALL_COLS - 3592, D_MODEL), w_in.dtype)], axis=1).astype(BF16)
    lam = lax.complex(lam_re.astype(F32), lam_im.astype(F32))
    dt = jnp.exp(s5_log_dt.astype(F32))[:, :, None]
    a_bar = jnp.exp(lam * dt)
    b_bar = ((a_bar - 1.0) / lam)[..., None] * lax.complex(b_re.astype(F32), b_im.astype(F32))
    bt = jnp.swapaxes(b_bar, 2, 3)
    bcat = jnp.concatenate([_block_diag(jnp.real(bt)), _block_diag(jnp.imag(bt))], axis=2).astype(BF16)
    ct_re = jnp.swapaxes(c_re.astype(F32), 2, 3)
    ct_im = jnp.swapaxes(c_im.astype(F32), 2, 3)
    ccat = jnp.concatenate([_block_diag(ct_re), -_block_diag(ct_im)], axis=1).astype(BF16)
    lb = jnp.cumsum(jax.nn.softmax(hgrn_lb_logits.astype(F32), axis=0), axis=0)
    lb = lb - lb[0]
    hlb = jnp.stack([jnp.log(lb), jnp.log1p(-lb), 1.0 - lb] + [jnp.zeros_like(lb)] * 5, axis=1)
    mlb = jnp.concatenate([mlstm_b_i.astype(F32), mlstm_b_f.astype(F32), jnp.zeros((depth, 120), F32)], axis=1)
    nw = norm_w.astype(F32)[:, None, :]
    wo = w_out.astype(BF16)
    fnw = jnp.broadcast_to(final_norm_w.astype(F32)[None, None, :], (depth, 1, D_MODEL))
    a_re, a_im = jnp.real(a_bar).reshape(depth, S5_N), jnp.imag(a_bar).reshape(depth, S5_N)
    wglu = s5_w_glu.astype(BF16)
    prompt_weights = (nw, w_all, wo, fnw, a_re[:, None, :], a_im[:, None, :], bcat, ccat,
                      s5_d.astype(F32)[:, None, :], wglu, hlb, mlb[:, None, :])

    def on_lanes(v):
        return jnp.broadcast_to(v[..., None], v.shape + (SAMPLE_LANES,))
    sample_weights = (nw, w_all, wo, fnw, on_lanes(a_re), on_lanes(a_im),
                      jnp.swapaxes(bcat, 1, 2), jnp.swapaxes(ccat, 1, 2), on_lanes(s5_d.astype(F32)),
                      jnp.swapaxes(wglu, 1, 2), on_lanes(hlb[:, 0:3, :]), on_lanes(mlb[:, 0:8]))
    return prompt_weights, sample_weights


def _prompt_trunk(x, states, weights):
    s5re, s5im, hg, mc, mn, mm, rt = states
    d, b = s5re.shape[0], s5re.shape[1]
    assert x.shape[1] % PROMPT_T == 0 and x.shape[1] // PROMPT_T > 1 and (b * PROMPT_T) % CHUNK_ROWS == 0
    kstates = (s5re.reshape(d, b, S5_N), s5im.reshape(d, b, S5_N), hg, mc, mn.reshape(d, b, 256), mm, rt)
    cos_tab, sin_tab = _rotary_tables(jnp.arange(x.shape[1], dtype=jnp.int32))
    out = None
    for l in range(DEPTH):
        x, out = _run_layer(l, x, cos_tab, sin_tab, kstates, out, weights, NB=b, T=PROMPT_T)
    o_s5re, o_s5im, o_hg, o_mc, o_mn, o_mm, o_rt = out
    return x, [
        o_s5re.reshape(d, b, S5_GROUPS, S5_STATE), o_s5im.reshape(d, b, S5_GROUPS, S5_STATE),
        o_hg, o_mc, o_mn.reshape(d, b, N_HEADS, HEAD_W), o_mm, o_rt,
    ]


def _sample_trunk(x, states, weights):
    s5re, s5im, hg, mc, mn, mm, rt = states
    d, b, ls = s5re.shape[0], s5re.shape[1], x.shape[1]
    kstates = (
        jnp.transpose(s5re, (0, 2, 3, 1)).reshape(d, S5_N, b), jnp.transpose(s5im, (0, 2, 3, 1)).reshape(d, S5_N, b),
        jnp.transpose(hg, (0, 2, 3, 4, 1)), jnp.transpose(mc, (0, 2, 3, 4, 1)),
        jnp.transpose(mn, (0, 2, 3, 1)).reshape(d, N_HEADS * HEAD_W, b), jnp.transpose(mm, (0, 2, 1)),
        jnp.transpose(rt, (0, 2, 3, 4, 1)),
    )
    half = RET_DK // 2
    inv_freq = ROPE_BASE ** (-jnp.arange(half, dtype=F32) / half)
    ang = (PAST_LEN + jnp.arange(ls, dtype=jnp.int32)).astype(F32)[:, None] * inv_freq[None, :]
    tables = (jnp.broadcast_to(jnp.cos(ang)[:, :, None], (ls, half, b)),
              jnp.broadcast_to(jnp.sin(ang)[:, :, None], (ls, half, b)))
    xt = jnp.transpose(x, (1, 0, 2))
    out = None
    for l in range(DEPTH):
        xt, out = _run_sample_layer(l, xt, kstates, out, weights + tables)
    o_s5re, o_s5im, o_hg, o_mc, o_mn, o_mm, o_rt = out
    return jnp.transpose(xt, (1, 0, 2)), [
        jnp.transpose(o_s5re.reshape(d, S5_GROUPS, S5_STATE, b), (0, 3, 1, 2)),
        jnp.transpose(o_s5im.reshape(d, S5_GROUPS, S5_STATE, b), (0, 3, 1, 2)),
        jnp.transpose(o_hg, (0, 4, 1, 2, 3)), jnp.transpose(o_mc, (0, 4, 1, 2, 3)),
        jnp.transpose(o_mn.reshape(d, N_HEADS, HEAD_W, b), (0, 3, 1, 2)), jnp.transpose(o_mm, (0, 2, 1)),
        jnp.transpose(o_rt, (0, 4, 1, 2, 3)),
    ]


def kernel(x_prompt, x_sample, state_s5_re, state_s5_im, state_hgrn, state_mlstm_C, state_mlstm_n,
           state_mlstm_m, state_ret, norm_w, w_in, w_out, final_norm_w, s5_lambda_re, s5_lambda_im,
           s5_B_re, s5_B_im, s5_C_re, s5_C_im, s5_D, s5_log_dt, s5_w_glu, hgrn_lb_logits,
           mlstm_b_i, mlstm_b_f):
    prompt_weights, sample_weights = _prepare_weights(
        norm_w, w_in, w_out, final_norm_w, s5_lambda_re, s5_lambda_im, s5_B_re, s5_B_im,
        s5_C_re, s5_C_im, s5_D, s5_log_dt, s5_w_glu, hgrn_lb_logits, mlstm_b_i, mlstm_b_f)
    bp = x_prompt.shape[0]
    zero_states = (
        jnp.zeros((DEPTH, bp, S5_GROUPS, S5_STATE), F32), jnp.zeros((DEPTH, bp, S5_GROUPS, S5_STATE), F32),
        jnp.zeros((DEPTH, bp, N_HEADS, HEAD_W, HEAD_W), F32), jnp.zeros((DEPTH, bp, N_HEADS, HEAD_W, HEAD_W), F32),
        jnp.zeros((DEPTH, bp, N_HEADS, HEAD_W), F32), jnp.zeros((DEPTH, bp, N_HEADS), F32),
        jnp.zeros((DEPTH, bp, N_HEADS, RET_DK, HEAD_W), F32),
    )
    y_prompt, p_states = _prompt_trunk(x_prompt, zero_states, prompt_weights)
    sample_states = (state_s5_re, state_s5_im, state_hgrn, state_mlstm_C, state_mlstm_n, state_mlstm_m, state_ret)
    y_sample, d_states = _sample_trunk(x_sample, sample_states, sample_weights)
    return (y_prompt, y_sample, *p_states, *d_states)
```

```python
import functools
import math

import jax
import jax.numpy as jnp
from jax import lax
from jax.experimental import pallas as pl
from jax.experimental.pallas import tpu as pltpu

F32 = jnp.float32
BF16 = jnp.bfloat16

D_MODEL = 1024
DEPTH = 4
BRANCH_W = 256
N_HEADS = 4
HEAD_W = 64
S5_GROUPS = 16
S5_CH = 16
S5_STATE = 64
S5_N = S5_GROUPS * S5_STATE
S5_TILES = S5_N // 128
RET_DK = 32
RET_KW = N_HEADS * RET_DK
ROPE_BASE = 10000.0
EPS = 1e-6
PAST_LEN = 16384

CHUNK_ROWS = 128
STACK = N_HEADS * CHUNK_ROWS
PROMPT_T = 64
VMEM_LIMIT_BYTES = 60 * 1024 * 1024

C_S5 = 0
C_HG = 512
C_ML = 1536
C_RT = 2816
C_GATE = 3584
W_ALL_COLS = 3712
HGRN_SAFE_DECAY = 80.0


def _dot(a, b):
    return jnp.dot(a, b, preferred_element_type=F32)


def _dot_nt(a, b):
    return lax.dot_general(a, b, (((1,), (1,)), ((), ())), preferred_element_type=F32)


def _dot_tn(a, b):
    return lax.dot_general(a, b, (((0,), (0,)), ((), ())), preferred_element_type=F32)


def _split3(x):
    hi = x.astype(BF16)
    r = x - hi.astype(F32)
    mid = r.astype(BF16)
    lo = (r - mid.astype(F32)).astype(BF16)
    return hi, mid, lo


def _sel_dot(sel_bf, x):
    hi, mid, lo = _split3(x)
    return _dot(sel_bf, hi) + _dot(sel_bf, mid) + _dot(sel_bf, lo)


def _sel_dot2(sel_bf, x):
    hi = x.astype(BF16)
    lo = (x - hi.astype(F32)).astype(BF16)
    return _dot(sel_bf, hi) + _dot(sel_bf, lo)


def _dot_sel(x, sel_bf):
    hi = x.astype(BF16)
    lo = (x - hi.astype(F32)).astype(BF16)
    return _dot(hi, sel_bf) + _dot(lo, sel_bf)


def _sigmoid(x):
    return 0.5 + 0.5 * jnp.tanh(0.5 * x)


def _silu(x):
    return x * _sigmoid(x)


def _softplus_neg_abs(x):
    return jnp.log(1.0 + jnp.exp(-jnp.abs(x)))


def _log_sigmoid(x):
    return jnp.minimum(x, 0.0) - _softplus_neg_abs(x)


def _gelu_tanh(x):
    return 0.5 * x * (1.0 + jnp.tanh(math.sqrt(2.0 / math.pi) * (x + 0.044715 * (x * x * x))))


def _iota(shape, dim):
    return lax.broadcasted_iota(jnp.int32, shape, dim)


def _log2(n):
    k = int(round(math.log2(n)))
    assert (1 << k) == n, n
    return k


def _stack_heads(x_bf, head_mask):
    return jnp.where(head_mask, jnp.concatenate([x_bf] * N_HEADS, axis=0), jnp.zeros((), x_bf.dtype))


def _tile_state(s, mask):
    return jnp.where(mask, jnp.concatenate([s] * N_HEADS, axis=1), 0.0)


def _untile_state(s_bd):
    return (s_bd[:, 0:64] + s_bd[:, 64:128]) + (s_bd[:, 128:192] + s_bd[:, 192:256])


def _lanes_per_head(cols):
    return jnp.concatenate([jnp.broadcast_to(c, (CHUNK_ROWS, HEAD_W)) for c in cols], axis=1)


def _ret_log_decay(head):
    out = jnp.full(head.shape, math.log(1.0 - 2.0 ** -5.0), F32)
    for hh in range(1, N_HEADS):
        out = jnp.where(head == hh, math.log(1.0 - 2.0 ** (-5.0 - hh)), out)
    return out


N_STATES = 7
N_INPUTS = 23
N_SAMPLE_INPUTS = 22
SAMPLE_LANES = 128


def _layer_kernel(*refs, NB, T, final_norm, n_alias):
    (x_ref, xn_ref, cos_ref, sin_ref, s5re_in, s5im_in, hg_in, mc_in, mn_in, mm_in, rt_in,
     nw_ref, w_ref, wo_ref, fnw_ref, are_ref, aim_ref, bcat_ref, ccat_ref, dsk_ref,
     wglu_ref, hlb_ref, mlb_ref) = refs[:N_INPUTS]
    (y_ref, s5re_out, s5im_out, hg_out, mc_out, mn_out, mm_out, rt_out,
     pj_s, pjn_s, hn_s, bu_s, mix_s, hre_s, him_s, sthg_s, stmc_s, strt_s, stn_s, stm_s,
     perm_s, permt_s, sc_s) = refs[N_INPUTS + n_alias:]
    R = NB * T
    NCH = R // CHUNK_ROWS
    nseg = CHUNK_ROWS // T
    last = T - 1
    lgT = _log2(T)
    lgNB = _log2(NB)
    CR = CHUNK_ROWS
    j = pl.program_id(1)
    nj = pl.num_programs(1)

    bd_mask = (_iota((256, 256), 0) >> 6) == (_iota((256, 256), 1) >> 6)
    rt_mask = (_iota((RET_KW, 256), 0) >> 5) == (_iota((RET_KW, 256), 1) >> 6)
    hm256 = (_iota((STACK, 256), 0) >> 7) == (_iota((STACK, 256), 1) >> 6)
    hm128 = (_iota((STACK, RET_KW), 0) >> 7) == (_iota((STACK, RET_KW), 1) >> 5)
    bd_ones = bd_mask.astype(BF16)
    head_expand = hm256.astype(BF16)

    t_st = _iota((CR, STACK), 0)
    s_st = _iota((CR, STACK), 1) & (CR - 1)
    causal_st = ((t_st >> lgT) == (s_st >> lgT)) & (t_st >= s_st)
    diag_st = t_st == s_st
    t_sq = _iota((CR, CR), 0)
    s_sq = _iota((CR, CR), 1)
    causal_sq = ((t_sq >> lgT) == (s_sq >> lgT)) & (t_sq >= s_sq)
    triseg = causal_sq.astype(BF16)
    tl128 = _iota((CR, 128), 0) & (T - 1)
    row256 = _iota((CR, 256), 0)

    @pl.when(j == 0)
    def _():
        ri = _iota((R, R), 0)
        ci = _iota((R, R), 1)
        perm_s[...] = (ci == (((ri & (NB - 1)) << lgT) + (ri >> lgNB))).astype(BF16)
        permt_s[...] = (ri == (((ci & (NB - 1)) << lgT) + (ci >> lgNB))).astype(BF16)
        hre_s[...] = s5re_in[...]
        him_s[...] = s5im_in[...]
        stn_s[...] = mn_in[...]
        m_in = mm_in[...]
        stm_s[...] = jnp.concatenate(
            [jnp.broadcast_to(m_in[:, hh:hh + 1], (NB, HEAD_W)) for hh in range(N_HEADS)], axis=1)
        for b in range(NB):
            sthg_s[b] = jnp.transpose(_tile_state(hg_in[b].reshape(256, HEAD_W), bd_mask))
            stmc_s[b] = _tile_state(mc_in[b].reshape(256, HEAD_W), bd_mask)
            strt_s[b] = _tile_state(rt_in[b].reshape(RET_KW, HEAD_W), rt_mask)

    def project(src_ref, dst_ref):
        xs = src_ref[...].reshape(R, D_MODEL)
        ms = jnp.mean(xs * xs, axis=-1, keepdims=True)
        hn = (xs * lax.rsqrt(ms + EPS) * nw_ref[...]).astype(BF16)
        for i in range(C_GATE // 256):
            dst_ref[:, 256 * i:256 * (i + 1)] = _dot_nt(hn, w_ref[256 * i:256 * (i + 1), :])
        dst_ref[:, C_GATE:W_ALL_COLS] = _dot_nt(hn, w_ref[C_GATE:W_ALL_COLS, :])

    @pl.when(j == 0)
    def _():
        project(x_ref, pj_s)

    xs = xn_ref[...].reshape(R, D_MODEL)
    ms_n = jnp.mean(xs * xs, axis=-1, keepdims=True)
    hn_s[...] = (xs * lax.rsqrt(ms_n + EPS) * nw_ref[...]).astype(BF16)

    def proj_tile(c0, c1):
        def task():
            pjn_s[:, c0:c1] = _dot_nt(hn_s[...], w_ref[c0:c1, :])
        return task

    fillers = [proj_tile(256 * i, 256 * (i + 1)) for i in range(C_GATE // 256)]
    fillers.append(proj_tile(C_GATE, W_ALL_COLS))

    def fill(n=1):
        for _ in range(min(n, len(fillers))):
            fillers.pop(0)()

    def head_norm(o):
        mu = _dot_sel(o, bd_ones) * (1.0 / HEAD_W)
        d = o - mu
        var = _dot((d * d).astype(BF16), bd_ones) * (1.0 / HEAD_W)
        return d * lax.rsqrt(var + EPS)

    def for_chunks(fn):
        for ch in range(NCH):
            fill()
            fn(ch)

    def seg_last_rows(a):
        w = a.shape[1]
        pieces = [jnp.broadcast_to(a[sg * T + last:sg * T + last + 1, :], (T, w)) for sg in range(nseg)]
        return pieces[0] if nseg == 1 else jnp.concatenate(pieces, axis=0)

    def chunk_rows(ch):
        if isinstance(ch, int):
            return pl.ds(ch * CR, CR)
        return pl.ds(pl.multiple_of(ch * CR, CR), CR)

    a_re = are_ref[...]
    a_im = aim_ref[...]

    def s5_drive():
        u = pj_s[:, C_S5:C_S5 + 256]
        ub = _dot(perm_s[...], u.astype(BF16)).astype(BF16)
        for i in range(S5_TILES):
            drive = _dot(ub, bcat_ref[:, 256 * i:256 * (i + 1)])
            bu_s[2 * i] = drive[:, 0:128]
            bu_s[2 * i + 1] = drive[:, 128:256]

    def s5_step(t, carry):
        idx = pl.ds(t * NB, NB)
        new = []
        for c in range(S5_TILES):
            hre, him = carry[2 * c], carry[2 * c + 1]
            ar = a_re[:, 128 * c:128 * (c + 1)]
            ai = a_im[:, 128 * c:128 * (c + 1)]
            nre = ar * hre - ai * him + bu_s[c, idx, :]
            nim = ar * him + ai * hre + bu_s[S5_TILES + c, idx, :]
            bu_s[c, idx, :] = nre
            bu_s[S5_TILES + c, idx, :] = nim
            new += [nre, nim]
        return tuple(new)

    def s5_scan():
        fin = []
        for c in range(S5_TILES):
            fin += [hre_s[:, 128 * c:128 * (c + 1)], him_s[:, 128 * c:128 * (c + 1)]]
        fin = tuple(fin)
        for t in range(T):
            fin = s5_step(t, fin)
        for c in range(S5_TILES):
            hre_s[:, 128 * c:128 * (c + 1)] = fin[2 * c]
            him_s[:, 128 * c:128 * (c + 1)] = fin[2 * c + 1]

    def s5_output():
        u = pj_s[:, C_S5:C_S5 + 256]
        hid = jnp.concatenate([bu_s[c].astype(BF16) for c in range(2 * S5_TILES)], axis=1)
        y = _sel_dot2(permt_s[...], _dot(hid, ccat_ref[...])) + dsk_ref[...] * u
        fill()
        y = _gelu_tanh(y)
        y = y * _sigmoid(_dot(y.astype(BF16), wglu_ref[...]))
        mix_s[:, 0:256] = (y * _silu(pj_s[:, C_S5 + 256:C_S5 + 512])).astype(BF16)

    log_lb = hlb_ref[0:1, :]
    log1m_lb = hlb_ref[1:2, :]
    one_m_lb = hlb_ref[2:3, :]

    def boundary_rows(c, m):
        blk = 2 * m
        if blk >= 8:
            c3 = c.reshape(CR // blk, blk, 256)
            return jnp.broadcast_to(c3[:, m - 1:m, :], (CR // blk, blk, 256)).reshape(CR, 256)
        c3 = c.reshape(CR // 8, 8, 256)
        sub = _iota((CR // 8, 8, 256), 1)
        out = None
        for jb in range(8 // blk):
            piece = jnp.broadcast_to(c3[:, jb * blk + m - 1:jb * blk + m, :], (CR // 8, 8, 256))
            out = piece if out is None else jnp.where((sub >> _log2(blk)) == jb, piece, out)
        return out.reshape(CR, 256)

    def hgrn_gates(rows):
        q = _silu(pj_s[rows, C_HG:C_HG + 256])
        xf = pj_s[rows, C_HG + 256:C_HG + 512]
        b_ = log1m_lb + _log_sigmoid(xf)
        log_f = jnp.maximum(log_lb, b_) + _softplus_neg_abs(log_lb - b_)
        nlf = -log_f
        hk = one_m_lb * _sigmoid(-xf)
        return q, hk, _sel_dot(triseg, nlf)

    def hgrn_scores_any_decay(q, hk, c):
        scores = jnp.where(diag_st, _dot_nt(q.astype(BF16), _stack_heads(hk.astype(BF16), hm256)), 0.0)
        m = T // 2
        while m >= 1:
            e = jnp.exp(-jnp.abs(c - boundary_rows(c, m)))
            second = (row256 & (2 * m - 1)) >= m
            qe = jnp.where(second, q * e, 0.0).astype(BF16)
            ke = jnp.where(second, 0.0, hk * e).astype(BF16)
            lg = _log2(2 * m)
            same_blk = (t_st >> lg) == (s_st >> lg)
            scores = scores + jnp.where(same_blk, _dot_nt(qe, _stack_heads(ke, hm256)), 0.0)
            m //= 2
        return scores

    s5_drive()
    fill()
    s5_scan()
    s5_output()
    fill()
    gates = [hgrn_gates(chunk_rows(ch)) for ch in range(NCH)]
    c_max = gates[0][2]
    for ch in range(1, NCH):
        c_max = jnp.maximum(c_max, gates[ch][2])
    c_max = jnp.max(c_max)
    for ch in range(NCH):
        q, hk, c = gates[ch]
        ke = _stack_heads((hk * jnp.exp(c)).astype(BF16), hm256)
        sc_s[ch] = jnp.where(causal_st, _dot_nt((q * jnp.exp(-c)).astype(BF16), ke), 0.0).astype(BF16)

    @pl.when(c_max > HGRN_SAFE_DECAY)
    def _():
        def redo(ch, carry):
            q, hk, c = hgrn_gates(chunk_rows(ch))
            sc_s[ch] = hgrn_scores_any_decay(q, hk, c).astype(BF16)
            return carry
        lax.fori_loop(0, NCH, redo, 0)

    def hgrn_chunk(ch):
        rows = chunk_rows(ch)
        q, hk, c = gates[ch]
        v = pj_s[rows, C_HG + 512:C_HG + 768].astype(BF16)
        gate = pj_s[rows, C_HG + 768:C_HG + 1024]
        o = _dot(sc_s[ch], _stack_heads(v, hm256))

        qd = (q * jnp.exp(-c)).astype(BF16)
        kd = (hk * jnp.exp(-(seg_last_rows(c) - c))).astype(BF16)
        o_inter = []
        for sg in range(nseg):
            b = ch * nseg + sg
            sl = slice(sg * T, (sg + 1) * T)
            st = sthg_s[b]
            o_inter.append(_dot_nt(qd[sl], st.astype(BF16)))
            c_last = c[sg * T + last:sg * T + last + 1, :]
            upd = _dot_tn(v[sl], kd[sl])
            sthg_s[b] = st * jnp.exp(-c_last) + jnp.where(bd_mask, upd, 0.0)
        o = o + (o_inter[0] if nseg == 1 else jnp.concatenate(o_inter, axis=0))
        mix_s[rows, 256:512] = (head_norm(o) * _silu(gate)).astype(BF16)

    for_chunks(hgrn_chunk)

    lane128 = _iota((CR, 128), 1)
    is_i = lane128 < N_HEADS
    is_f = (lane128 >= N_HEADS) & (lane128 < 2 * N_HEADS)

    def mlstm_chunk(ch):
        rows = chunk_rows(ch)
        q = pj_s[rows, C_ML:C_ML + 256]
        k = pj_s[rows, C_ML + 256:C_ML + 512] * (HEAD_W ** -0.5)
        v = pj_s[rows, C_ML + 512:C_ML + 768].astype(BF16)
        og = pj_s[rows, C_ML + 768:C_ML + 1024]
        gate = pj_s[rows, C_ML + 1024:C_ML + 1280]
        gpre = pj_s[rows, C_GATE:W_ALL_COLS] + mlb_ref[...]
        log_i = gpre
        log_fg = _log_sigmoid(gpre)
        cum =_sel_dot(triseg, jnp.where(is_f, log_fg, 0.0))
        g = jnp.where(is_i, log_i, cum)
        g_t = jnp.transpose(g)

        if nseg == 1:
            ms_rows = jnp.broadcast_to(stm_s[pl.ds(ch, 1), :], (CR, 256))
        else:
            ms_rows = jnp.concatenate(
                [jnp.broadcast_to(stm_s[pl.ds(ch * nseg + sg, 1), :], (T, 256)) for sg in range(nseg)], axis=0)

        d_blocks, mt_cols, wp_cols, cf_cols, ig_cols = [], [], [], [], []
        for hh in range(N_HEADS):
            cf_col = g[:, N_HEADS + hh:N_HEADS + hh + 1]
            ig_col = g[:, hh:hh + 1]
            row_f = jnp.broadcast_to(g_t[N_HEADS + hh:N_HEADS + hh + 1, :], (CR, CR))
            row_i = jnp.broadcast_to(g_t[hh:hh + 1, :], (CR, CR))
            log_d = jnp.where(causal_sq, jnp.broadcast_to(cf_col, (CR, CR)) - row_f + row_i, -jnp.inf)
            w_prev = cf_col + ms_rows[:, hh * HEAD_W:hh * HEAD_W + 1]
            m_t = jnp.maximum(w_prev, jnp.max(log_d, axis=1, keepdims=True))
            d_blocks.append(jnp.exp(log_d - m_t))
            mt_cols.append(m_t)
            wp_cols.append(w_prev)
            cf_cols.append(cf_col)
            ig_cols.append(ig_col)
        d_mat = jnp.concatenate(d_blocks, axis=1)
        mt = _lanes_per_head(mt_cols)
        s_prev = jnp.exp(_lanes_per_head(wp_cols) - mt)
        cumf = _lanes_per_head(cf_cols)
        lig = _lanes_per_head(ig_cols)

        qb = q.astype(BF16)
        scores = _dot_nt(qb, _stack_heads(k.astype(BF16), hm256)) * d_mat
        num = _dot(scores.astype(BF16), _stack_heads(v, hm256))
        den = _dot_sel(scores, head_expand)

        w_k = jnp.exp(seg_last_rows(cumf) - cumf + lig - seg_last_rows(mt))
        kw = w_k * k
        kwb = kw.astype(BF16)
        qc, qn = [], []
        for sg in range(nseg):
            b = ch * nseg + sg
            sl = slice(sg * T, (sg + 1) * T)
            cs = stmc_s[b]
            ns = stn_s[pl.ds(b, 1), :]
            ms_b = stm_s[pl.ds(b, 1), :]
            qc.append(_dot(qb[sl], cs.astype(BF16)))
            qn.append(_dot_sel(q[sl] * ns, bd_ones))
            lr = sg * T + last
            m_new = mt[lr:lr + 1, :]
            carry = jnp.exp(cumf[lr:lr + 1, :] + ms_b - m_new)
            stmc_s[b] = carry * cs + jnp.where(bd_mask, _dot_tn(kwb[sl], v[sl]), 0.0)
            stn_s[pl.ds(b, 1), :] = carry * ns + jnp.sum(kw[sl], axis=0, keepdims=True)
            stm_s[pl.ds(b, 1), :] = m_new
        qc = qc[0] if nseg == 1 else jnp.concatenate(qc, axis=0)
        qn = qn[0] if nseg == 1 else jnp.concatenate(qn, axis=0)
        num = s_prev * qc + num
        den = s_prev * qn + den
        hcell = num / jnp.maximum(jnp.abs(den), jnp.exp(-mt))
        mix_s[rows, 512:768] = (head_norm(_sigmoid(og) * hcell) * _silu(gate)).astype(BF16)

    for_chunks(mlstm_chunk)

    gam_st = _ret_log_decay(_iota((CR, STACK), 1) >> 7)
    d_ret = jnp.where(causal_st, jnp.exp(((t_st - s_st) & (T - 1)).astype(F32) * gam_st), 0.0)
    gam128 = _ret_log_decay(_iota((CR, 128), 1) >> 5)
    gam256 = _ret_log_decay(_iota((1, 256), 1) >> 6)
    q_dec = jnp.exp((tl128 + 1).astype(F32) * gam128)
    k_dec = jnp.exp((last - tl128).astype(F32) * gam128)
    s_dec = jnp.exp(float(last + 1) * gam256)
    first_half = (_iota((CR, 128), 1) & (RET_DK - 1)) < (RET_DK // 2)

    def rotary(t, cos, sin_signed):
        partner = jnp.where(first_half, pltpu.roll(t, 128 - RET_DK // 2, axis=1), pltpu.roll(t, RET_DK // 2, axis=1))
        return t * cos + partner * sin_signed

    def ret_chunk(ch):
        rows = chunk_rows(ch)
        cos = jnp.concatenate([cos_ref[...]] * nseg, axis=0)
        sin_signed = jnp.concatenate([sin_ref[...]] * nseg, axis=0)
        q = rotary(pj_s[rows, C_RT:C_RT + 128], cos, sin_signed)
        k = rotary(pj_s[rows, C_RT + 128:C_RT + 256], cos, sin_signed) * (RET_DK ** -0.5)
        v = pj_s[rows, C_RT + 256:C_RT + 512].astype(BF16)
        gate = pj_s[rows, C_RT + 512:C_RT + 768]
        scores = _dot_nt(q.astype(BF16), _stack_heads(k.astype(BF16), hm128)) * d_ret
        o = _dot(scores.astype(BF16), _stack_heads(v, hm256))
        qd = (q * q_dec).astype(BF16)
        kd = (k * k_dec).astype(BF16)
        o_inter = []
        for sg in range(nseg):
            b = ch * nseg + sg
            sl = slice(sg * T, (sg + 1) * T)
            st = strt_s[b]
            o_inter.append(_dot(qd[sl], st.astype(BF16)))
            strt_s[b] = st * s_dec + jnp.where(rt_mask, _dot_tn(kd[sl], v[sl]), 0.0)
        o = o + (o_inter[0] if nseg == 1 else jnp.concatenate(o_inter, axis=0))
        mix_s[rows, 768:1024] = (head_norm(o) * _silu(gate)).astype(BF16)

    for_chunks(ret_chunk)

    fill(len(fillers))
    out = x_ref[...].reshape(R, D_MODEL) + _dot(mix_s[...], wo_ref[...])
    if final_norm:
        ms2 = jnp.mean(out * out, axis=-1, keepdims=True)
        out = out * lax.rsqrt(ms2 + EPS) * fnw_ref[...]
    y_ref[...] = out.reshape(NB, T, D_MODEL)

    @pl.when(j == nj - 1)
    def _():
        s5re_out[...] = hre_s[...]
        s5im_out[...] = him_s[...]
        mn_out[...] = stn_s[...]
        mm_out[...] = jnp.concatenate([stm_s[:, hh * HEAD_W:hh * HEAD_W + 1] for hh in range(N_HEADS)], axis=1)
        for b in range(NB):
            hg_out[b] = _untile_state(jnp.transpose(sthg_s[b])).reshape(N_HEADS, HEAD_W, HEAD_W)
            mc_out[b] = _untile_state(stmc_s[b]).reshape(N_HEADS, HEAD_W, HEAD_W)
            rt_out[b] = _untile_state(strt_s[b]).reshape(N_HEADS, RET_DK, HEAD_W)

    pj_s[...] = pjn_s[...]


def _run_layer(l, x, cos_tab, sin_tab, states, prev_out, weights, *, NB, T):
    B, L, _ = x.shape
    nbb, ntc = B // NB, L // T
    R = NB * T

    def layer_spec(shape, first, single_buffer=False):
        rest = tuple(shape[2:])
        zeros = (0,) * len(rest)
        kw = {"pipeline_mode": pl.Buffered(1)} if single_buffer else {}
        if first is None:
            return pl.BlockSpec((None, shape[1]) + rest, lambda i, j: (l, 0) + zeros, **kw)
        return pl.BlockSpec((None, first) + rest, lambda i, j: (l, i) + zeros, **kw)

    state_specs = [layer_spec(s.shape, NB) for s in states]
    in_specs = [
        pl.BlockSpec((NB, T, D_MODEL), lambda i, j: (i, j, 0)),
        pl.BlockSpec((NB, T, D_MODEL), lambda i, j: (i, jnp.minimum(j + 1, ntc - 1), 0)),
        pl.BlockSpec((T, 128), lambda i, j: (j, 0)),
        pl.BlockSpec((T, 128), lambda i, j: (j, 0)),
    ] + state_specs + [layer_spec(w.shape, None, single_buffer=True) for w in weights]
    assert len(in_specs) == N_INPUTS
    operands = [x, x, cos_tab, sin_tab, *states, *weights]
    aliases = {}
    if prev_out is not None:
        in_specs += [pl.BlockSpec(memory_space=pl.ANY)] * N_STATES
        aliases = {N_INPUTS + k: 1 + k for k in range(N_STATES)}
        operands += list(prev_out)
    out_shape = [jax.ShapeDtypeStruct(x.shape, F32)] + [jax.ShapeDtypeStruct(s.shape, F32) for s in states]
    out_specs = [pl.BlockSpec((NB, T, D_MODEL), lambda i, j: (i, j, 0))] + state_specs
    scratch = [
        pltpu.VMEM((R, W_ALL_COLS), F32),
        pltpu.VMEM((R, W_ALL_COLS), F32),
        pltpu.VMEM((R, D_MODEL), BF16),
        pltpu.VMEM((2 * S5_TILES, R, 128), F32),
        pltpu.VMEM((R, D_MODEL), BF16),
        pltpu.VMEM((NB, S5_N), F32),
        pltpu.VMEM((NB, S5_N), F32),
        pltpu.VMEM((NB, 256, 256), F32),
        pltpu.VMEM((NB, 256, 256), F32),
        pltpu.VMEM((NB, RET_KW, 256), F32),
        pltpu.VMEM((NB, 256), F32),
        pltpu.VMEM((NB, 256), F32),
        pltpu.VMEM((R, R), BF16),
        pltpu.VMEM((R, R), BF16),
        pltpu.VMEM((R // CHUNK_ROWS, CHUNK_ROWS, STACK), BF16),
    ]
    kern = functools.partial(_layer_kernel, NB=NB, T=T, final_norm=(l == DEPTH - 1), n_alias=len(aliases))
    outs = pl.pallas_call(
        kern,
        grid=(nbb, ntc),
        in_specs=in_specs,
        out_specs=out_specs,
        out_shape=out_shape,
        scratch_shapes=scratch,
        input_output_aliases=aliases,
        compiler_params=pltpu.CompilerParams(
            dimension_semantics=("arbitrary", "arbitrary"),
            vmem_limit_bytes=VMEM_LIMIT_BYTES,
        ),
    )(*operands)
    return outs[0], outs[1:]


def _sample_kernel(*refs, L, final_norm, n_alias):
    (x_ref, s5re_in, s5im_in, hg_in, mc_in, mn_in, mm_in, rt_in,
     nw_ref, wt_ref, wo_ref, fnw_ref, are_ref, aim_ref, bcatt_ref, ccatt_ref, dsk_ref, wglut_ref,
     hlb_ref, mlb_ref, cos_ref, sin_ref) = refs[:N_SAMPLE_INPUTS]
    (y_ref, s5re_out, s5im_out, hg_out, mc_out, mn_out, mm_out, rt_out,
     pj_s, bu_s, hid_s, mix_s, a_s, b_s, c_s, d_s) = refs[N_SAMPLE_INPUTS + n_alias:]
    h = pl.program_id(0)
    nh = pl.num_programs(0)
    B = SAMPLE_LANES
    half = RET_DK // 2

    def cols(t):
        return slice(t * B, (t + 1) * B)

    def tile_tokens(p):
        return jnp.concatenate([p] * L, axis=1)

    @pl.when(h == 0)
    def _():
        hs = []
        for t in range(L):
            xt = x_ref[t]
            ms = jnp.mean(xt * xt, axis=-1, keepdims=True)
            hs.append((xt * lax.rsqrt(ms + EPS) * nw_ref[...]).astype(BF16))
        hb = jnp.concatenate(hs, axis=0)
        for r0 in range(0, W_ALL_COLS, 256):
            r1 = min(r0 + 256, W_ALL_COLS)
            pj_s[r0:r1, :] = _dot_nt(wt_ref[r0:r1, :], hb)

        u = pj_s[C_S5:C_S5 + 256, :]
        bu_s[...] = _dot(bcatt_ref[...], u.astype(BF16))
        hre, him = s5re_in[...].reshape(S5_N, B), s5im_in[...].reshape(S5_N, B)
        a_re, a_im = are_ref[...], aim_ref[...]
        for t in range(L):
            nre = a_re * hre - a_im * him + bu_s[0:S5_N, cols(t)]
            nim = a_re * him + a_im * hre + bu_s[S5_N:2 * S5_N, cols(t)]
            hid_s[0:S5_N, cols(t)] = nre.astype(BF16)
            hid_s[S5_N:2 * S5_N, cols(t)] = nim.astype(BF16)
            hre, him = nre, nim
        s5re_out[...] = hre.reshape(S5_GROUPS, S5_STATE, B)
        s5im_out[...] = him.reshape(S5_GROUPS, S5_STATE, B)
        y = _dot(ccatt_ref[...], hid_s[...]) + tile_tokens(dsk_ref[...]) * u
        y = _gelu_tanh(y)
        y = y * _sigmoid(_dot(wglut_ref[...], y.astype(BF16)))
        mix_s[0:256, :] = (y * _silu(pj_s[C_S5 + 256:C_S5 + 512, :])).astype(BF16)

    def head_rows(base, width=HEAD_W):
        return pl.ds(pl.multiple_of(base + h * width, width), width)

    def head_norm_rows(o):
        mu = jnp.mean(o, axis=0, keepdims=True)
        d = o - mu
        return d * lax.rsqrt(jnp.mean(d * d, axis=0, keepdims=True) + EPS)

    def row_on_sublanes(ref, t, k):
        return jnp.broadcast_to(ref[t, pl.ds(k, 1), :], (HEAD_W, B))

    def recur(nk, st_in, st_out, decay, key_s, query_s, value_s):
        def body(k, outs):
            s = st_in[k]
            new = []
            for t in range(L):
                s = decay(t, k) * s + row_on_sublanes(key_s, t, k) * value_s[t]
                new.append(outs[t] + row_on_sublanes(query_s, t, k) * s)
            st_out[k] = s
            return tuple(new)
        return lax.fori_loop(0, nk, body, (jnp.zeros((HEAD_W, B), F32),) * L, unroll=2)

    hrows = pl.ds(pl.multiple_of(h * HEAD_W, HEAD_W), HEAD_W)
    log_lb, log1m_lb, one_m_lb = hlb_ref[0, hrows, :], hlb_ref[1, hrows, :], hlb_ref[2, hrows, :]
    for t in range(L):
        xf = pj_s[head_rows(C_HG + 256), cols(t)]
        b_ = log1m_lb + _log_sigmoid(xf)
        log_f = jnp.maximum(log_lb, b_) + _softplus_neg_abs(log_lb - b_)
        a_s[t] = jnp.exp(log_f)
        b_s[t] = one_m_lb * _sigmoid(-xf)
        c_s[t] = _silu(pj_s[head_rows(C_HG), cols(t)])
        d_s[t] = pj_s[head_rows(C_HG + 512), cols(t)]
    outs = recur(HEAD_W, hg_in, hg_out, lambda t, k: row_on_sublanes(a_s, t, k), b_s, c_s, d_s)
    for t in range(L):
        gate = pj_s[head_rows(C_HG + 768), cols(t)]
        mix_s[head_rows(256), cols(t)] = (head_norm_rows(outs[t]) * _silu(gate)).astype(BF16)

    m_prev = mm_in[pl.ds(h, 1), :]
    n_vec = mn_in[...]
    gate_row = _iota((2 * N_HEADS, B), 0)
    dens, inv_caps = [], []
    for t in range(L):
        gates = pj_s[C_GATE:C_GATE + 2 * N_HEADS, cols(t)] + mlb_ref[...]
        log_i = jnp.sum(jnp.where(gate_row == h, gates, 0.0), axis=0, keepdims=True)
        log_fg = _log_sigmoid(jnp.sum(jnp.where(gate_row == h + N_HEADS, gates, 0.0), axis=0, keepdims=True))
        m_t = jnp.maximum(log_fg + m_prev, log_i)
        f_dec = jnp.exp(log_fg + m_prev - m_t)
        q = pj_s[head_rows(C_ML), cols(t)]
        kw = jnp.exp(log_i - m_t) * (pj_s[head_rows(C_ML + 256), cols(t)] * (HEAD_W ** -0.5))
        n_vec = f_dec * n_vec + kw
        dens.append(jnp.sum(q * n_vec, axis=0, keepdims=True))
        inv_caps.append(jnp.exp(-m_t))
        a_s[t] = jnp.broadcast_to(f_dec, (HEAD_W, B))
        b_s[t] = kw
        c_s[t] = q
        d_s[t] = pj_s[head_rows(C_ML + 512), cols(t)]
        m_prev = m_t
    outs = recur(HEAD_W, mc_in, mc_out, lambda t, k: a_s[t], b_s, c_s, d_s)
    mn_out[...] = n_vec
    mm_out[pl.ds(h, 1), :] = m_prev
    for t in range(L):
        cell = outs[t] / jnp.maximum(jnp.abs(dens[t]), inv_caps[t])
        og = pj_s[head_rows(C_ML + 768), cols(t)]
        gate = pj_s[head_rows(C_ML + 1024), cols(t)]
        mix_s[head_rows(512), cols(t)] = (head_norm_rows(_sigmoid(og) * cell) * _silu(gate)).astype(BF16)

    decay = jnp.float32(1.0 - 2.0 ** -5.0)
    for hh in range(1, N_HEADS):
        decay = jnp.where(h == hh, jnp.float32(1.0 - 2.0 ** (-5.0 - hh)), decay)
    for t in range(L):
        cos, sin = cos_ref[t], sin_ref[t]
        for base, dst, scale in ((C_RT, c_s, 1.0), (C_RT + RET_KW, b_s, RET_DK ** -0.5)):
            t1 = pj_s[pl.ds(pl.multiple_of(base + h * RET_DK, half), half), cols(t)] * scale
            t2 = pj_s[pl.ds(pl.multiple_of(base + h * RET_DK + half, half), half), cols(t)] * scale
            dst[t, 0:half, :] = t1 * cos - t2 * sin
            dst[t, half:RET_DK, :] = t1 * sin + t2 * cos
        d_s[t] = pj_s[head_rows(C_RT + 256), cols(t)]
    outs = recur(RET_DK, rt_in, rt_out, lambda t, k: decay, b_s, c_s, d_s)
    for t in range(L):
        gate = pj_s[head_rows(C_RT + 512), cols(t)]
        mix_s[head_rows(768), cols(t)] = (head_norm_rows(outs[t]) * _silu(gate)).astype(BF16)

    @pl.when(h == nh - 1)
    def _():
        yn = _dot_tn(mix_s[...], wo_ref[...])
        for t in range(L):
            out = x_ref[t] + yn[t * B:(t + 1) * B, :]
            if final_norm:
                ms2 = jnp.mean(out * out, axis=-1, keepdims=True)
                out = out * lax.rsqrt(ms2 + EPS) * fnw_ref[...]
            y_ref[t] = out


def _run_sample_layer(l, x, states, prev_out, weights):
    L, B, _ = x.shape
    assert B == SAMPLE_LANES

    def lspec(shape, per_head, single_buffer=False):
        kw = {"pipeline_mode": pl.Buffered(1)} if single_buffer else {}
        if per_head == "axis":
            rest = tuple(shape[2:])
            return pl.BlockSpec((None, None) + rest, lambda h: (l, h) + (0,) * len(rest), **kw)
        rest = tuple(shape[1:])
        return pl.BlockSpec((None,) + rest, lambda h: (l,) + (0,) * len(rest), **kw)

    s5re, s5im, hg, mc, mn, mm, rt = states
    state_specs = [lspec(s5re.shape, None), lspec(s5im.shape, None), lspec(hg.shape, "axis"), lspec(mc.shape, "axis"),
                   lspec(mn.shape, "axis"), lspec(mm.shape, None), lspec(rt.shape, "axis")]
    (nw, wt, wo, fnw, are_b, aim_b, bcatt, ccatt, dsk_b, wglut, hlb_b, mlb_b, cos_t, sin_t) = weights
    table_spec = pl.BlockSpec(cos_t.shape, lambda h: (0, 0, 0))
    in_specs = ([pl.BlockSpec(x.shape, lambda h: (0, 0, 0))] + state_specs
                + [lspec(w.shape, None, single_buffer=True) for w in weights[:-2]] + [table_spec, table_spec])
    assert len(in_specs) == N_SAMPLE_INPUTS
    operands = [x, *states, *weights]
    aliases = {}
    if prev_out is not None:
        in_specs += [pl.BlockSpec(memory_space=pl.ANY)] * N_STATES
        aliases = {N_SAMPLE_INPUTS + k: 1 + k for k in range(N_STATES)}
        operands += list(prev_out)
    n_cols = L * B
    scratch = [
        pltpu.VMEM((W_ALL_COLS, n_cols), F32),
        pltpu.VMEM((2 * S5_N, n_cols), F32),
        pltpu.VMEM((2 * S5_N, n_cols), BF16),
        pltpu.VMEM((D_MODEL, n_cols), BF16),
    ] + [pltpu.VMEM((L, HEAD_W, B), F32)] * 4
    kern = functools.partial(_sample_kernel, L=L, final_norm=(l == DEPTH - 1), n_alias=len(aliases))
    outs = pl.pallas_call(
        kern,
        grid=(N_HEADS,),
        in_specs=in_specs,
        out_specs=[pl.BlockSpec(x.shape, lambda h: (0, 0, 0))] + state_specs,
        out_shape=[jax.ShapeDtypeStruct(x.shape, F32)] + [jax.ShapeDtypeStruct(s.shape, F32) for s in states],
        scratch_shapes=scratch,
        input_output_aliases=aliases,
        compiler_params=pltpu.CompilerParams(
            dimension_semantics=("arbitrary",),
            vmem_limit_bytes=VMEM_LIMIT_BYTES,
        ),
    )(*operands)
    return outs[0], outs[1:]


def _rotary_tables(pos):
    half = RET_DK // 2
    inv_freq = ROPE_BASE ** (-jnp.arange(half, dtype=F32) / half)
    ang = pos.astype(F32)[:, None] * inv_freq[None, :]
    cos, sin = jnp.cos(ang), jnp.sin(ang)
    cos_t = jnp.tile(cos, (1, 128 // half))
    sin_t = jnp.tile(jnp.concatenate([-sin, sin], axis=1), (1, 128 // RET_DK))
    return cos_t, sin_t


def _block_diag(blocks):
    d, g, a, b = blocks.shape
    eye = jnp.eye(g, dtype=blocks.dtype)
    return jnp.einsum('dgab,gh->dgahb', blocks, eye).reshape(d, g * a, g * b)


def _prepare_weights(norm_w, w_in, w_out, final_norm_w, lam_re, lam_im, b_re, b_im, c_re, c_im,
                     s5_d, s5_log_dt, s5_w_glu, hgrn_lb_logits, mlstm_b_i, mlstm_b_f):
    depth = w_in.shape[0]
    w_t = jnp.swapaxes(w_in, 1, 2)
    w_all = jnp.concatenate(
        [w_t[:, 0:2560], w_t[:, 2568:3592], w_t[:, 2560:2568],
         jnp.zeros((depth, W_ALL_COLS - 3592, D_MODEL), w_in.dtype)], axis=1).astype(BF16)
    lam = lax.complex(lam_re.astype(F32), lam_im.astype(F32))
    dt = jnp.exp(s5_log_dt.astype(F32))[:, :, None]
    a_bar = jnp.exp(lam * dt)
    b_bar = ((a_bar - 1.0) / lam)[..., None] * lax.complex(b_re.astype(F32), b_im.astype(F32))
    bt = jnp.swapaxes(b_bar, 2, 3)
    bcat = jnp.concatenate([_block_diag(jnp.real(bt)), _block_diag(jnp.imag(bt))], axis=2).astype(BF16)
    ct_re = jnp.swapaxes(c_re.astype(F32), 2, 3)
    ct_im = jnp.swapaxes(c_im.astype(F32), 2, 3)
    ccat = jnp.concatenate([_block_diag(ct_re), -_block_diag(ct_im)], axis=1).astype(BF16)
    lb = jnp.cumsum(jax.nn.softmax(hgrn_lb_logits.astype(F32), axis=0), axis=0)
    lb = lb - lb[0]
    hlb = jnp.stack([jnp.log(lb), jnp.log1p(-lb), 1.0 - lb] + [jnp.zeros_like(lb)] * 5, axis=1)
    mlb = jnp.concatenate([mlstm_b_i.astype(F32), mlstm_b_f.astype(F32), jnp.zeros((depth, 120), F32)], axis=1)
    nw = norm_w.astype(F32)[:, None, :]
    wo = w_out.astype(BF16)
    fnw = jnp.broadcast_to(final_norm_w.astype(F32)[None, None, :], (depth, 1, D_MODEL))
    a_re, a_im = jnp.real(a_bar).reshape(depth, S5_N), jnp.imag(a_bar).reshape(depth, S5_N)
    wglu = s5_w_glu.astype(BF16)
    prompt_weights = (nw, w_all, wo, fnw, a_re[:, None, :], a_im[:, None, :], bcat, ccat,
                      s5_d.astype(F32)[:, None, :], wglu, hlb, mlb[:, None, :])

    def on_lanes(v):
        return jnp.broadcast_to(v[..., None], v.shape + (SAMPLE_LANES,))
    sample_weights = (nw, w_all, wo, fnw, on_lanes(a_re), on_lanes(a_im),
                      jnp.swapaxes(bcat, 1, 2), jnp.swapaxes(ccat, 1, 2), on_lanes(s5_d.astype(F32)),
                      jnp.swapaxes(wglu, 1, 2), on_lanes(hlb[:, 0:3, :]), on_lanes(mlb[:, 0:8]))
    return prompt_weights, sample_weights


def _prompt_trunk(x, states, weights):
    s5re, s5im, hg, mc, mn, mm, rt = states
    d, b = s5re.shape[0], s5re.shape[1]
    assert x.shape[1] % PROMPT_T == 0 and x.shape[1] // PROMPT_T > 1 and (b * PROMPT_T) % CHUNK_ROWS == 0
    kstates = (s5re.reshape(d, b, S5_N), s5im.reshape(d, b, S5_N), hg, mc, mn.reshape(d, b, 256), mm, rt)
    cos_tab, sin_tab = _rotary_tables(jnp.arange(x.shape[1], dtype=jnp.int32))
    out = None
    for l in range(DEPTH):
        x, out = _run_layer(l, x, cos_tab, sin_tab, kstates, out, weights, NB=b, T=PROMPT_T)
    o_s5re, o_s5im, o_hg, o_mc, o_mn, o_mm, o_rt = out
    return x, [
        o_s5re.reshape(d, b, S5_GROUPS, S5_STATE), o_s5im.reshape(d, b, S5_GROUPS, S5_STATE),
        o_hg, o_mc, o_mn.reshape(d, b, N_HEADS, HEAD_W), o_mm, o_rt,
    ]


def _sample_trunk(x, states, weights):
    s5re, s5im, hg, mc, mn, mm, rt = states
    d, b, ls = s5re.shape[0], s5re.shape[1], x.shape[1]
    kstates = (
        jnp.transpose(s5re, (0, 2, 3, 1)), jnp.transpose(s5im, (0, 2, 3, 1)),
        jnp.transpose(hg, (0, 2, 3, 4, 1)), jnp.transpose(mc, (0, 2, 3, 4, 1)),
        jnp.transpose(mn, (0, 2, 3, 1)), jnp.transpose(mm, (0, 2, 1)),
        jnp.transpose(rt, (0, 2, 3, 4, 1)),
    )
    half = RET_DK // 2
    inv_freq = ROPE_BASE ** (-jnp.arange(half, dtype=F32) / half)
    ang = (PAST_LEN + jnp.arange(ls, dtype=jnp.int32)).astype(F32)[:, None] * inv_freq[None, :]
    tables = (jnp.broadcast_to(jnp.cos(ang)[:, :, None], (ls, half, b)),
              jnp.broadcast_to(jnp.sin(ang)[:, :, None], (ls, half, b)))
    xt = jnp.transpose(x, (1, 0, 2))
    out = None
    for l in range(DEPTH):
        xt, out = _run_sample_layer(l, xt, kstates, out, weights + tables)
    o_s5re, o_s5im, o_hg, o_mc, o_mn, o_mm, o_rt = out
    return jnp.transpose(xt, (1, 0, 2)), [
        jnp.transpose(o_s5re, (0, 3, 1, 2)), jnp.transpose(o_s5im, (0, 3, 1, 2)),
        jnp.transpose(o_hg, (0, 4, 1, 2, 3)), jnp.transpose(o_mc, (0, 4, 1, 2, 3)),
        jnp.transpose(o_mn, (0, 3, 1, 2)), jnp.transpose(o_mm, (0, 2, 1)),
        jnp.transpose(o_rt, (0, 4, 1, 2, 3)),
    ]


def kernel(x_prompt, x_sample, state_s5_re, state_s5_im, state_hgrn, state_mlstm_C, state_mlstm_n,
           state_mlstm_m, state_ret, norm_w, w_in, w_out, final_norm_w, s5_lambda_re, s5_lambda_im,
           s5_B_re, s5_B_im, s5_C_re, s5_C_im, s5_D, s5_log_dt, s5_w_glu, hgrn_lb_logits,
           mlstm_b_i, mlstm_b_f):
    prompt_weights, sample_weights = _prepare_weights(
        norm_w, w_in, w_out, final_norm_w, s5_lambda_re, s5_lambda_im, s5_B_re, s5_B_im,
        s5_C_re, s5_C_im, s5_D, s5_log_dt, s5_w_glu, hgrn_lb_logits, mlstm_b_i, mlstm_b_f)
    bp = x_prompt.shape[0]
    zero_states = (
        jnp.zeros((DEPTH, bp, S5_GROUPS, S5_STATE), F32), jnp.zeros((DEPTH, bp, S5_GROUPS, S5_STATE), F32),
        jnp.zeros((DEPTH, bp, N_HEADS, HEAD_W, HEAD_W), F32), jnp.zeros((DEPTH, bp, N_HEADS, HEAD_W, HEAD_W), F32),
        jnp.zeros((DEPTH, bp, N_HEADS, HEAD_W), F32), jnp.zeros((DEPTH, bp, N_HEADS), F32),
        jnp.zeros((DEPTH, bp, N_HEADS, RET_DK, HEAD_W), F32),
    )
    y_prompt, p_states = _prompt_trunk(x_prompt, zero_states, prompt_weights)
    sample_states = (state_s5_re, state_s5_im, state_hgrn, state_mlstm_C, state_mlstm_n, state_mlstm_m, state_ret)
    y_sample, d_states = _sample_trunk(x_sample, sample_states, sample_weights)
    return (y_prompt, y_sample, *p_states, *d_states)
```
